```python
import jax, jax.numpy as jnp
from jax import lax
import numpy as np

D_MODEL = 2048
BATCH = 2
SEQ = 4096
DEPTH = 4
DEC_BATCH = 8
DEC_SEQ = 1
PAST_LEN = 16384
PAGE_SIZE = 128

N_MIXERS = 3
N_LAYERS_A = (DEPTH + 2) // 3
N_LAYERS_B = (DEPTH + 1) // 3
N_LAYERS_C = DEPTH // 3
N_META = 16
ALPHA = (2 * DEPTH) ** 0.25
OUT_INIT_SCALE = (8 * DEPTH) ** -0.25
LN_EPS = 1e-5
RMS_EPS = 1e-6
D_FF = 4 * D_MODEL
CONV_A = 3
HEAD_DIM = 128
N_HEADS = D_MODEL // HEAD_DIM
N_KV_HEADS = 4
N_IDX_HEADS = 16
D_IDX = 64
TOPK_MAX = 256
Q_BLOCK = 128
ROPE_THETA = 10000.0
DSA_Q = N_HEADS * HEAD_DIM
DSA_KV = N_KV_HEADS * HEAD_DIM
DSA_QI = N_IDX_HEADS * D_IDX
DSA_SPLITS = [DSA_Q, DSA_Q + DSA_KV, DSA_Q + 2 * DSA_KV, DSA_Q + 2 * DSA_KV + DSA_QI, DSA_Q + 2 * DSA_KV + DSA_QI + D_IDX]
DSA_IN = DSA_SPLITS[-1] + N_IDX_HEADS
DN_QK_HEADS = 16
DN_V_HEADS = 32
DN_DK = 128
DN_DV = 128
DN_CONV = 4
DN_CHUNK = 64
DN_QK = DN_QK_HEADS * DN_DK
DN_VW = DN_V_HEADS * DN_DV
DN_CONV_DIM = 2 * DN_QK + DN_VW
DN_SPLITS = [DN_CONV_DIM, DN_CONV_DIM + DN_VW, DN_CONV_DIM + DN_VW + DN_V_HEADS]
DN_IN = DN_SPLITS[-1] + DN_V_HEADS

kernel_name = 'dsa_deltanet_shortconv_hybrid_step'


def layer_norm(x, g, b):
    xf = x.astype(jnp.float32)
    mu = jnp.mean(xf, -1, keepdims=True)
    var = jnp.mean(jnp.square(xf - mu), -1, keepdims=True)
    return ((xf - mu) * lax.rsqrt(var + LN_EPS) * g + b).astype(x.dtype)


def l2norm(x):
    return x * lax.rsqrt(jnp.sum(jnp.square(x), -1, keepdims=True) + RMS_EPS)


def rope(x, pos):
    d = x.shape[-1]
    inv = ROPE_THETA ** (-jnp.arange(0, d, 2, dtype=jnp.float32) / d)
    ang = pos.astype(jnp.float32)[:, None] * inv[None, :]
    cos = jnp.cos(ang)[:, None, :]
    sin = jnp.sin(ang)[:, None, :]
    x1, x2 = jnp.split(x.astype(jnp.float32), 2, axis=-1)
    return jnp.concatenate([x1 * cos - x2 * sin, x2 * cos + x1 * sin], axis=-1).astype(x.dtype)


def causal_dwconv(u_ext, w):
    width = w.shape[0]
    t = u_ext.shape[1] - (width - 1)
    return sum(u_ext[:, j:j + t] * w[j] for j in range(width))


def take_rows(a, idx):
    return jax.vmap(lambda a_b, i_b: a_b[i_b])(a, idx)


def sq_relu_mlp(x, w1, w2):
    return jnp.square(jax.nn.relu(x @ w1)) @ w2


def shortconv_mixer(x, conv_hist, w_in, conv_w, w_out):
    b_gate, c_gate, h = jnp.split(x @ w_in, 3, axis=-1)
    u_ext = jnp.concatenate([conv_hist, c_gate * h], axis=1)
    y = b_gate * causal_dwconv(u_ext, conv_w)
    return y @ w_out, u_ext[:, -(CONV_A - 1):]


def dsa_project(x, w_in, pos):
    bn, t, _ = x.shape
    q, k, v, qi, ki, wi = jnp.split(x @ w_in, DSA_SPLITS, axis=-1)
    q = rope(q.reshape(bn, t, N_HEADS, HEAD_DIM), pos)
    k = rope(k.reshape(bn, t, N_KV_HEADS, HEAD_DIM), pos)
    v = v.reshape(bn, t, N_KV_HEADS, HEAD_DIM)
    qi = rope(qi.reshape(bn, t, N_IDX_HEADS, D_IDX), pos)
    ki = rope(ki[:, :, None, :], pos)[:, :, 0]
    return q, k, v, qi, ki, wi


def indexer_scores(qi, wi, ki):
    dots = jax.nn.relu(jnp.einsum('bthd,bsd->bths', qi, ki).astype(jnp.float32) * D_IDX ** -0.5)
    return jnp.einsum('bths,bth->bts', dots, wi.astype(jnp.float32) * N_IDX_HEADS ** -0.5)


def sparse_attend(q, k_sel, v_sel, valid):
    bn, t = q.shape[:2]
    qg = q.reshape(bn, t, N_KV_HEADS, N_HEADS // N_KV_HEADS, HEAD_DIM)
    s = jnp.einsum('btngd,btsnd->btngs', qg, k_sel).astype(jnp.float32) * HEAD_DIM ** -0.5
    s = jnp.where(valid[:, :, None, None, :], s, -jnp.inf)
    p = jax.nn.softmax(s, axis=-1).astype(v_sel.dtype)
    o = jnp.einsum('btngs,btsnd->btngd', p, v_sel)
    return o.reshape(bn, t, N_HEADS * HEAD_DIM)


def dsa_prompt(x, w_in, w_out):
    bn, t, _ = x.shape
    pos = jnp.arange(t, dtype=jnp.int32)
    q, k, v, qi, ki, wi = dsa_project(x, w_in, pos)
    k_keep = min(TOPK_MAX, (t - N_META) // 4)
    n_blk = -(-t // Q_BLOCK)
    t_pad = n_blk * Q_BLOCK - t

    def blocks(a):
        a = jnp.pad(a, ((0, 0), (0, t_pad)) + ((0, 0),) * (a.ndim - 2))
        return jnp.moveaxis(a.reshape((bn, n_blk, Q_BLOCK) + a.shape[2:]), 1, 0)

    pos_blk = jnp.arange(n_blk * Q_BLOCK, dtype=jnp.int32).reshape(n_blk, Q_BLOCK)

    def attend_block(args):
        q_b, qi_b, wi_b, pos_b = args
        sc = indexer_scores(qi_b, wi_b, ki)
        sc = jnp.where(pos[None, None, :] <= pos_b[None, :, None], sc, -jnp.inf)
        _, idx = lax.top_k(sc, k_keep)
        valid = idx <= pos_b[None, :, None]
        return sparse_attend(q_b, take_rows(k, idx), take_rows(v, idx), valid)

    o = lax.map(attend_block, (blocks(q), blocks(qi), blocks(wi), pos_blk))
    o = jnp.moveaxis(o, 0, 1).reshape(bn, n_blk * Q_BLOCK, N_HEADS * HEAD_DIM)[:, :t]
    return o @ w_out, k, v, ki


def dsa_sample(x, cache_k, cache_v, cache_ki, page_table, w_in, w_out):
    bn, t, _ = x.shape
    past = page_table.shape[1] * PAGE_SIZE
    pos = past + jnp.arange(t, dtype=jnp.int32)
    q, k, v, qi, ki, wi = dsa_project(x, w_in, pos)
    ki_all = jnp.concatenate([cache_ki[page_table].reshape(bn, past, D_IDX), ki], axis=1)
    k_keep = min(TOPK_MAX, (past + t) // 4)
    key_pos = jnp.arange(past + t, dtype=jnp.int32)
    sc = indexer_scores(qi, wi, ki_all)
    sc = jnp.where(key_pos[None, None, :] <= pos[None, :, None], sc, -jnp.inf)
    _, idx = lax.top_k(sc, k_keep)
    valid = idx <= pos[None, :, None]
    in_past = idx < past
    idx_past = jnp.minimum(idx, past - 1)
    phys = page_table[jnp.arange(bn)[:, None, None], idx_past // PAGE_SIZE]
    slot = idx_past % PAGE_SIZE
    idx_new = jnp.clip(idx - past, 0, t - 1)

    def select(cache, new):
        return jnp.where(in_past[..., None, None], cache[phys, slot], take_rows(new, idx_new))

    o = sparse_attend(q, select(cache_k, k), select(cache_v, v), valid)
    return o @ w_out, k, v, ki


def gated_delta_chunked(q, k, v, g, beta, s0):
    bn, t, h, _ = q.shape
    n_pad = (-t) % DN_CHUNK
    c = DN_CHUNK

    def prep(a):
        a = jnp.pad(a, ((0, 0), (n_pad, 0)) + ((0, 0),) * (a.ndim - 2))
        a = a.reshape((bn, a.shape[1] // c, c) + a.shape[2:])
        return jnp.moveaxis(a, 3, 2)

    q, k, v, g, beta = (prep(a) for a in (q, k, v, g, beta))
    gc = jnp.cumsum(g, axis=-1)
    causal = jnp.tril(jnp.ones((c, c), bool))
    strict = jnp.tril(jnp.ones((c, c), bool), -1)
    decay = jnp.exp(jnp.where(causal, gc[..., :, None] - gc[..., None, :], -jnp.inf))
    a_mat = jnp.where(strict, beta[..., :, None] * decay * jnp.einsum('bnhid,bnhjd->bnhij', k, k), 0.0)
    rhs = jnp.concatenate([beta[..., None] * v, (beta * jnp.exp(gc))[..., None] * k], axis=-1)
    sol = lax.linalg.triangular_solve(a_mat + jnp.eye(c, dtype=a_mat.dtype), rhs,
                                      left_side=True, lower=True, unit_diagonal=True)
    u, w = jnp.split(sol, [v.shape[-1]], axis=-1)
    p = jnp.einsum('bnhid,bnhjd->bnhij', q, k) * decay
    qg = q * jnp.exp(gc)[..., None]
    kd = k * jnp.exp(gc[..., -1:] - gc)[..., None]
    g_last = jnp.exp(gc[..., -1])

    def step(s, xs):
        u_c, w_c, p_c, qg_c, kd_c, gl_c = xs
        delta = u_c - jnp.einsum('bhck,bhkv->bhcv', w_c, s)
        o_c = jnp.einsum('bhck,bhkv->bhcv', qg_c, s) + jnp.einsum('bhij,bhjv->bhiv', p_c, delta)
        s = gl_c[..., None, None] * s + jnp.einsum('bhck,bhcv->bhkv', kd_c, delta)
        return s, o_c

    xs = tuple(jnp.moveaxis(a, 1, 0) for a in (u, w, p, qg, kd, g_last))
    s, o = lax.scan(step, s0, xs)
    o = jnp.moveaxis(jnp.moveaxis(o, 0, 1), 2, 3)
    o = o.reshape(bn, -1, h, o.shape[-1])[:, n_pad:]
    return o, s


def gated_delta_recurrent(q, k, v, g, beta, s0):
    def step(s, xs):
        q_t, k_t, v_t, g_t, b_t = xs
        s = jnp.exp(g_t)[..., None, None] * s
        v_old = jnp.einsum('bhkv,bhk->bhv', s, k_t)
        s = s + jnp.einsum('bhk,bhv->bhkv', k_t, b_t[..., None] * (v_t - v_old))
        return s, jnp.einsum('bhkv,bhk->bhv', s, q_t)

    xs = tuple(jnp.moveaxis(a, 1, 0) for a in (q, k, v, g, beta))
    s, o = lax.scan(step, s0, xs)
    return jnp.moveaxis(o, 0, 1), s


def deltanet_mixer(x, conv_hist, s0, w_in, conv_w, a_log, dt_bias, norm_w, w_out, chunked):
    bn, t, _ = x.shape
    qkv, z, b_raw, a_raw = jnp.split(x @ w_in, DN_SPLITS, axis=-1)
    qkv_ext = jnp.concatenate([conv_hist, qkv], axis=1)
    qkv_c = jax.nn.silu(causal_dwconv(qkv_ext, conv_w)).astype(jnp.float32)
    q, k, v = jnp.split(qkv_c, [DN_QK, 2 * DN_QK], axis=-1)
    rep = DN_V_HEADS // DN_QK_HEADS
    q = jnp.repeat(l2norm(q.reshape(bn, t, DN_QK_HEADS, DN_DK)), rep, axis=2) * DN_DK ** -0.5
    k = jnp.repeat(l2norm(k.reshape(bn, t, DN_QK_HEADS, DN_DK)), rep, axis=2)
    v = v.reshape(bn, t, DN_V_HEADS, DN_DV)
    beta = jax.nn.sigmoid(b_raw.astype(jnp.float32))
    g = -jnp.exp(a_log.astype(jnp.float32)) * jax.nn.softplus(a_raw.astype(jnp.float32) + dt_bias.astype(jnp.float32))
    core = gated_delta_chunked if chunked else gated_delta_recurrent
    o, s = core(q, k, v, g, beta, s0.astype(jnp.float32))
    o = o * lax.rsqrt(jnp.mean(jnp.square(o), -1, keepdims=True) + RMS_EPS) * norm_w.astype(jnp.float32)
    o = o * jax.nn.silu(z.astype(jnp.float32).reshape(bn, t, DN_V_HEADS, DN_DV))
    y = o.reshape(bn, t, DN_VW).astype(x.dtype) @ w_out
    return y, qkv_ext[:, -(DN_CONV - 1):], s.astype(s0.dtype)


def setup_inputs(seed: int = 0) -> dict:
    key = jax.random.key(seed)
    keys = iter(jax.random.split(key, 40))

    def nrm(shape, scale=1.0):
        return jax.random.normal(next(keys), shape, jnp.float32) * scale

    n_pages = PAST_LEN // PAGE_SIZE
    n_pool = (DEC_BATCH * n_pages * 5) // 4
    x_prompt = nrm((BATCH, SEQ, D_MODEL))
    x_sample = nrm((DEC_BATCH, DEC_SEQ, D_MODEL))
    state_conv_a = nrm((N_LAYERS_A, DEC_BATCH, CONV_A - 1, D_MODEL))
    cache_k = nrm((N_LAYERS_B, n_pool, PAGE_SIZE, N_KV_HEADS, HEAD_DIM))
    cache_v = nrm((N_LAYERS_B, n_pool, PAGE_SIZE, N_KV_HEADS, HEAD_DIM))
    cache_idx_k = nrm((N_LAYERS_B, n_pool, PAGE_SIZE, D_IDX))
    state_conv_c = nrm((N_LAYERS_C, DEC_BATCH, DN_CONV - 1, DN_CONV_DIM))
    state_delta = nrm((N_LAYERS_C, DEC_BATCH, DN_V_HEADS, DN_DK, DN_DV), 0.1)
    page_table = jax.random.permutation(next(keys), n_pool)[:DEC_BATCH * n_pages].reshape(DEC_BATCH, n_pages).astype(jnp.int32)
    meta_tokens = nrm((N_META, D_MODEL))
    w_in_a = nrm((N_LAYERS_A, D_MODEL, 3 * D_MODEL), D_MODEL ** -0.5)
    conv_w_a = nrm((N_LAYERS_A, CONV_A, D_MODEL), CONV_A ** -0.5)
    w_out_a = nrm((N_LAYERS_A, D_MODEL, D_MODEL), OUT_INIT_SCALE * D_MODEL ** -0.5)
    w_in_b = nrm((N_LAYERS_B, D_MODEL, DSA_IN), D_MODEL ** -0.5)
    w_out_b = nrm((N_LAYERS_B, DSA_Q, D_MODEL), OUT_INIT_SCALE * DSA_Q ** -0.5)
    w_in_c = nrm((N_LAYERS_C, D_MODEL, DN_IN), D_MODEL ** -0.5)
    conv_w_c = nrm((N_LAYERS_C, DN_CONV, DN_CONV_DIM), DN_CONV ** -0.5)
    a_log_c = jnp.log(jax.random.uniform(next(keys), (N_LAYERS_C, DN_V_HEADS), jnp.float32, 1.0, 16.0))
    dt_c = jax.random.uniform(next(keys), (N_LAYERS_C, DN_V_HEADS), jnp.float32, 0.001, 0.1)
    dt_bias_c = jnp.log(jnp.expm1(dt_c))
    norm_w_c = 1.0 + nrm((N_LAYERS_C, DN_DV), 0.01)
    w_out_c = nrm((N_LAYERS_C, DN_VW, D_MODEL), OUT_INIT_SCALE * DN_VW ** -0.5)
    ln1_g = 1.0 + nrm((DEPTH, D_MODEL), 0.01)
    ln1_b = nrm((DEPTH, D_MODEL), 0.01)
    w1 = nrm((DEPTH, D_MODEL, D_FF), D_MODEL ** -0.5)
    w2 = nrm((DEPTH, D_FF, D_MODEL), OUT_INIT_SCALE * D_FF ** -0.5)
    ln2_g = 1.0 + nrm((DEPTH, D_MODEL), 0.01)
    ln2_b = nrm((DEPTH, D_MODEL), 0.01)
    return {'x_prompt': x_prompt, 'x_sample': x_sample, 'state_conv_a': state_conv_a,
            'cache_k': cache_k, 'cache_v': cache_v, 'cache_idx_k': cache_idx_k,
            'state_conv_c': state_conv_c, 'state_delta': state_delta, 'page_table': page_table,
            'meta_tokens': meta_tokens, 'w_in_a': w_in_a, 'conv_w_a': conv_w_a, 'w_out_a': w_out_a,
            'w_in_b': w_in_b, 'w_out_b': w_out_b, 'w_in_c': w_in_c, 'conv_w_c': conv_w_c,
            'a_log_c': a_log_c, 'dt_bias_c': dt_bias_c, 'norm_w_c': norm_w_c, 'w_out_c': w_out_c,
            'ln1_g': ln1_g, 'ln1_b': ln1_b, 'w1': w1, 'w2': w2, 'ln2_g': ln2_g, 'ln2_b': ln2_b}


def reference(x_prompt, x_sample, state_conv_a, cache_k, cache_v, cache_idx_k, state_conv_c,
              state_delta, page_table, meta_tokens, w_in_a, conv_w_a, w_out_a, w_in_b, w_out_b,
              w_in_c, conv_w_c, a_log_c, dt_bias_c, norm_w_c, w_out_c, ln1_g, ln1_b, w1, w2,
              ln2_g, ln2_b):
    bp = x_prompt.shape[0]
    meta = jnp.broadcast_to(meta_tokens[None].astype(x_prompt.dtype), (bp, N_META, D_MODEL))
    xp = jnp.concatenate([meta, x_prompt], axis=1)
    xs = x_sample
    conv_a_p, conv_a_s = [], []
    k_p, v_p, ki_p, k_s, v_s, ki_s = [], [], [], [], [], []
    conv_c_p, conv_c_s, delta_p, delta_s = [], [], [], []
    for i in range(DEPTH):
        j = i // N_MIXERS
        kind = i % N_MIXERS
        if kind == 0:
            hp, cp = shortconv_mixer(xp, jnp.zeros((bp, CONV_A - 1, D_MODEL), xp.dtype), w_in_a[j], conv_w_a[j], w_out_a[j])
            hs, cs = shortconv_mixer(xs, state_conv_a[j], w_in_a[j], conv_w_a[j], w_out_a[j])
            conv_a_p.append(cp)
            conv_a_s.append(cs)
        elif kind == 1:
            hp, kp, vp, kip = dsa_prompt(xp, w_in_b[j], w_out_b[j])
            hs, ks, vs, kis = dsa_sample(xs, cache_k[j], cache_v[j], cache_idx_k[j], page_table, w_in_b[j], w_out_b[j])
            k_p.append(kp)
            v_p.append(vp)
            ki_p.append(kip)
            k_s.append(ks)
            v_s.append(vs)
            ki_s.append(kis)
        else:
            hp, cp, sp = deltanet_mixer(xp, jnp.zeros((bp, DN_CONV - 1, DN_CONV_DIM), xp.dtype),
                                        jnp.zeros((bp, DN_V_HEADS, DN_DK, DN_DV), xp.dtype),
                                        w_in_c[j], conv_w_c[j], a_log_c[j], dt_bias_c[j], norm_w_c[j], w_out_c[j], True)
            hs, cs, ss = deltanet_mixer(xs, state_conv_c[j], state_delta[j], w_in_c[j], conv_w_c[j],
                                        a_log_c[j], dt_bias_c[j], norm_w_c[j], w_out_c[j], False)
            conv_c_p.append(cp)
            conv_c_s.append(cs)
            delta_p.append(sp)
            delta_s.append(ss)
        xp = layer_norm(ALPHA * xp + hp, ln1_g[i], ln1_b[i])
        xs = layer_norm(ALPHA * xs + hs, ln1_g[i], ln1_b[i])
        xp = layer_norm(ALPHA * xp + sq_relu_mlp(xp, w1[i], w2[i]), ln2_g[i], ln2_b[i])
        xs = layer_norm(ALPHA * xs + sq_relu_mlp(xs, w1[i], w2[i]), ln2_g[i], ln2_b[i])
    return (xp[:, N_META:], xs,
            jnp.stack(conv_a_p), jnp.stack(conv_a_s),
            jnp.stack(k_p), jnp.stack(v_p), jnp.stack(ki_p),
            jnp.stack(k_s), jnp.stack(v_s), jnp.stack(ki_s),
            jnp.stack(conv_c_p), jnp.stack(conv_c_s),
            jnp.stack(delta_p), jnp.stack(delta_s))
```

```python
import functools

import jax
import jax.numpy as jnp
from jax import lax
from jax.experimental import pallas as pl
from jax.experimental.pallas import tpu as pltpu

F32 = jnp.float32
BF16 = jnp.bfloat16
I32 = jnp.int32

D = 2048
NB = 2
SEQ = 4096
DEPTH = 4
NS = 8
PAST = 16384
PAGE = 128
N_META = 16
T_REAL = N_META + SEQ
TP = 4224
R = NB * TP
S0 = T_REAL
ALPHA = (2 * DEPTH) ** 0.25
LN_EPS = 1e-5
RMS_EPS = 1e-6
D_FF = 4 * D
TM = 768

HEAD_DIM = 128
N_HEADS = 16
N_KV = 4
N_IDX_HEADS = 16
D_IDX = 64
TOPK = 256
ROPE_THETA = 10000.0
QB = 128
KC = 384

DN_QK_HEADS = 16
DN_V_HEADS = 32
DN_DK = 128
DN_DV = 128
DN_QK = DN_QK_HEADS * DN_DK
DN_VW = DN_V_HEADS * DN_DV
DN_CONV_DIM = 2 * DN_QK + DN_VW
CH = 64
NCH = TP // CH
HG = 4
NG = DN_V_HEADS // HG

VMEM_LIMIT = 58 * 1024 * 1024


def _cparams(sem):
    return pltpu.CompilerParams(dimension_semantics=sem, vmem_limit_bytes=VMEM_LIMIT)


def _layer_norm(y, g, b):
    mu = jnp.mean(y, -1, keepdims=True)
    yc = y - mu
    var = jnp.mean(jnp.square(yc), -1, keepdims=True)
    return yc * lax.rsqrt(var + LN_EPS) * g + b


def _mm_kernel(x_ref, w_ref, o_ref, wbf_ref):
    @pl.when(pl.program_id(1) == 0)
    def _():
        wbf_ref[...] = w_ref[...].astype(BF16)

    o_ref[...] = jnp.dot(x_ref[...].astype(BF16), wbf_ref[...],
                         preferred_element_type=F32).astype(o_ref.dtype)


def _project(x, w, layer, ncols, tn):
    rows, k = x.shape
    return pl.pallas_call(
        _mm_kernel,
        grid=(ncols // tn, rows // TM),
        in_specs=[pl.BlockSpec((TM, k), lambda j, i: (i, 0)),
                  pl.BlockSpec((None, k, tn), lambda j, i: (layer, 0, j))],
        out_specs=pl.BlockSpec((TM, tn), lambda j, i: (i, j)),
        out_shape=jax.ShapeDtypeStruct((rows, ncols), F32),
        scratch_shapes=[pltpu.VMEM((k, tn), BF16)],
        compiler_params=_cparams(("arbitrary", "arbitrary")),
        name="project",
    )(x, w)


def _project_tail(x, w, layer, col0):
    wt = w[layer, :, col0:]
    wt = jnp.pad(wt, ((0, 0), (0, 128 - wt.shape[1])))[None]
    return _project(x, wt, 0, 128, 128)


def _out_ln_kernel(a_ref, w_ref, r_ref, g_ref, b_ref, o_ref, *, nk):
    k = pl.program_id(1)
    part = jnp.dot(a_ref[...], w_ref[...].astype(BF16), preferred_element_type=F32)

    @pl.when(k == 0)
    def _():
        o_ref[...] = part

    @pl.when(k > 0)
    def _():
        o_ref[...] += part

    @pl.when(k == nk - 1)
    def _():
        y = ALPHA * r_ref[...] + o_ref[...]
        o_ref[...] = _layer_norm(y, g_ref[...], b_ref[...])


def _out_ln(a, w, wl, resid, g, b, gl):
    rows, k = a.shape
    tk = 512
    nk = k // tk
    return pl.pallas_call(
        functools.partial(_out_ln_kernel, nk=nk),
        grid=(rows // TM, nk),
        in_specs=[pl.BlockSpec((TM, tk), lambda i, kk: (i, kk)),
                  pl.BlockSpec((None, tk, D), lambda i, kk: (wl, kk, 0)),
                  pl.BlockSpec((TM, D), lambda i, kk: (i, 0)),
                  pl.BlockSpec((None, 1, D), lambda i, kk: (gl, 0, 0)),
                  pl.BlockSpec((None, 1, D), lambda i, kk: (gl, 0, 0))],
        out_specs=pl.BlockSpec((TM, D), lambda i, kk: (i, 0)),
        out_shape=jax.ShapeDtypeStruct((rows, D), F32),
        compiler_params=_cparams(("arbitrary", "arbitrary")),
        name="out_ln",
    )(a, w, resid, g.reshape(DEPTH, 1, D), b.reshape(DEPTH, 1, D))


def _mlp_kernel(x_ref, w1_ref, w2_ref, g_ref, b_ref, o_ref, xbf_ref, *, nf):
    f = pl.program_id(1)

    @pl.when(f == 0)
    def _():
        xbf_ref[...] = x_ref[...].astype(BF16)

    h = jnp.dot(xbf_ref[...], w1_ref[...].astype(BF16), preferred_element_type=F32)
    h = jnp.square(jnp.maximum(h, 0.0)).astype(BF16)
    part = jnp.dot(h, w2_ref[...].astype(BF16), preferred_element_type=F32)

    @pl.when(f == 0)
    def _():
        o_ref[...] = part

    @pl.when(f > 0)
    def _():
        o_ref[...] += part

    @pl.when(f == nf - 1)
    def _():
        y = ALPHA * x_ref[...] + o_ref[...]
        o_ref[...] = _layer_norm(y, g_ref[...], b_ref[...])


def _mlp_ln(x, w1, w2, g, b, layer):
    rows = x.shape[0]
    tf = 512
    nf = D_FF // tf
    return pl.pallas_call(
        functools.partial(_mlp_kernel, nf=nf),
        grid=(rows // TM, nf),
        in_specs=[pl.BlockSpec((TM, D), lambda i, f: (i, 0)),
                  pl.BlockSpec((None, D, tf), lambda i, f: (layer, 0, f)),
                  pl.BlockSpec((None, tf, D), lambda i, f: (layer, f, 0)),
                  pl.BlockSpec((None, 1, D), lambda i, f: (layer, 0, 0)),
                  pl.BlockSpec((None, 1, D), lambda i, f: (layer, 0, 0))],
        out_specs=pl.BlockSpec((TM, D), lambda i, f: (i, 0)),
        out_shape=jax.ShapeDtypeStruct((rows, D), F32),
        scratch_shapes=[pltpu.VMEM((TM, D), BF16)],
        compiler_params=_cparams(("arbitrary", "arbitrary")),
        name="mlp_ln",
    )(x, w1, w2, g.reshape(DEPTH, 1, D), b.reshape(DEPTH, 1, D))


CB_A = 128


def _conv_a_kernel(bg_ref, cg_ref, h_ref, w_ref, y_ref, st_ref):
    u = cg_ref[...] * h_ref[...]
    row = lax.broadcasted_iota(I32, u.shape, 0)
    u1 = jnp.where(row >= 1, pltpu.roll(u, 1, 0), 0.0)
    u2 = jnp.where(row >= 2, pltpu.roll(u, 2, 0), 0.0)
    w = w_ref[...]
    y = bg_ref[...] * (w[0:1] * u2 + w[1:2] * u1 + w[2:3] * u)
    y_ref[...] = y.astype(BF16)
    st_ref[...] = u[T_REAL - 8:T_REAL]


def _conv_a_prompt(p, conv_w, layer):
    p3 = p.reshape(NB, TP, 3 * D)
    nc = D // CB_A
    y, st = pl.pallas_call(
        _conv_a_kernel,
        grid=(NB, nc),
        in_specs=[pl.BlockSpec((None, TP, CB_A), lambda b, c: (b, 0, c)),
                  pl.BlockSpec((None, TP, CB_A), lambda b, c: (b, 0, nc + c)),
                  pl.BlockSpec((None, TP, CB_A), lambda b, c: (b, 0, 2 * nc + c)),
                  pl.BlockSpec((None, 3, CB_A), lambda b, c: (layer, 0, c))],
        out_specs=[pl.BlockSpec((None, TP, CB_A), lambda b, c: (b, 0, c)),
                   pl.BlockSpec((None, 8, CB_A), lambda b, c: (b, 0, c))],
        out_shape=[jax.ShapeDtypeStruct((NB, TP, D), BF16),
                   jax.ShapeDtypeStruct((NB, 8, D), F32)],
        compiler_params=_cparams(("arbitrary", "arbitrary")),
        name="conv_a_prompt",
    )(p3, p3, p3, conv_w)
    return y.reshape(R, D), st[:, 6:8]


def _conv_a_sample_kernel(p_ref, st_ref, w_ref, yin_ref, y_ref, nst_ref):
    del yin_ref
    p = p_ref[...]
    u = p[:, D:2 * D] * p[:, 2 * D:]
    w = w_ref[...]
    y = p[:, :D] * (w[0:1] * st_ref[0] + w[1:2] * st_ref[1] + w[2:3] * u)
    y_ref[...] = jnp.concatenate([y, jnp.zeros_like(y)], axis=0).astype(BF16)
    nst_ref[0] = st_ref[1]
    nst_ref[1] = u


def _conv_a_sample(p, state, conv_w, layer, y):
    st_t = jnp.swapaxes(state, 0, 1)
    y, nst = pl.pallas_call(
        _conv_a_sample_kernel,
        grid=(1,),
        in_specs=[pl.BlockSpec((NS, 3 * D), lambda i: (S0 // NS, 0)),
                  pl.BlockSpec((2, NS, D), lambda i: (0, 0, 0)),
                  pl.BlockSpec((None, 3, D), lambda i: (layer, 0, 0)),
                  pl.BlockSpec(memory_space=pl.ANY)],
        out_specs=[pl.BlockSpec((16, D), lambda i: (S0 // 16, 0)),
                   pl.BlockSpec((2, NS, D), lambda i: (0, 0, 0))],
        out_shape=[jax.ShapeDtypeStruct((R, D), BF16),
                   jax.ShapeDtypeStruct((2, NS, D), F32)],
        input_output_aliases={3: 0},
        compiler_params=_cparams(("arbitrary",)),
        name="conv_a_sample",
    )(p, st_t, conv_w, y)
    return y, jnp.swapaxes(nst, 0, 1)


def _mixer_a(x, state, w_in, conv_w, layer):
    p = _project(x, w_in, layer, 3 * D, 1024)
    y, st_p = _conv_a_prompt(p, conv_w, layer)
    y, st_s = _conv_a_sample(p, state, conv_w, layer, y)
    return y, st_p, st_s


DSA_Q = N_HEADS * HEAD_DIM
DSA_KV = N_KV * HEAD_DIM
DSA_QI = N_IDX_HEADS * D_IDX
DSA_MAIN = DSA_Q + 2 * DSA_KV + DSA_QI
TR = 384
NKC = TP // KC
INT_MIN = -2 ** 31
IDX_SCALE = D_IDX ** -0.5 * N_IDX_HEADS ** -0.5
NEG_BIG = -1e30


def _rope_tables(pos):
    def tab(d):
        inv = ROPE_THETA ** (-jnp.arange(0, d, 2, dtype=F32) / d)
        ang = pos.astype(F32)[:, None] * inv[None, :]
        c, s = jnp.cos(ang), jnp.sin(ang)
        return jnp.concatenate([c, c], -1), jnp.concatenate([-s, s], -1)
    c128, s128 = tab(HEAD_DIM)
    c64, s64 = tab(D_IDX)
    return c128, s128, jnp.concatenate([c64, c64], -1), jnp.concatenate([s64, s64], -1)


def _rot128(x):
    return pltpu.roll(x, 64, 1)


def _rot64(x):
    lane = lax.broadcasted_iota(I32, x.shape, 1)
    return jnp.where((lane & 32) == 0, pltpu.roll(x, 96, 1), pltpu.roll(x, 32, 1))


def _rope_kernel(q_ref, k_ref, v_ref, qi_ref, t_ref, c128_ref, s128_ref, c64_ref, s64_ref,
                 qo_ref, kf_ref, kb_ref, vb_ref, qio_ref, kw_ref):
    c128, s128 = c128_ref[...], s128_ref[...]
    c64, s64 = c64_ref[...], s64_ref[...]
    for h in range(N_HEADS):
        sl = slice(h * 128, (h + 1) * 128)
        x = q_ref[:, sl]
        qo_ref[:, sl] = (x * c128 + _rot128(x) * s128).astype(qo_ref.dtype)
    for h in range(N_KV):
        sl = slice(h * 128, (h + 1) * 128)
        x = k_ref[:, sl]
        r = x * c128 + _rot128(x) * s128
        kf_ref[:, sl] = r
        kb_ref[:, sl] = r.astype(BF16)
    vb_ref[...] = v_ref[...].astype(BF16)
    for p in range(DSA_QI // 128):
        sl = slice(p * 128, (p + 1) * 128)
        x = qi_ref[:, sl]
        qio_ref[:, sl] = (x * c64 + _rot64(x) * s64).astype(qio_ref.dtype)
    t = t_ref[...]
    lane = lax.broadcasted_iota(I32, t.shape, 1)
    kw_ref[...] = jnp.where(lane < D_IDX, t * c64 + _rot64(t) * s64, t)


def _rope_prompt(p, tail):
    pos = jnp.arange(TP, dtype=I32)
    tabs = _rope_tables(pos)
    nt = TP // TR
    tab_spec = pl.BlockSpec((TR, 128), lambda i: (i % nt, 0))
    row = lambda w, j: pl.BlockSpec((TR, w), lambda i: (i, j))
    return pl.pallas_call(
        _rope_kernel,
        grid=(R // TR,),
        in_specs=[row(DSA_Q, 0), row(DSA_KV, 4), row(DSA_KV, 5), row(DSA_QI, 3), row(128, 0),
                  tab_spec, tab_spec, tab_spec, tab_spec],
        out_specs=[row(DSA_Q, 0), row(DSA_KV, 0), row(DSA_KV, 0), row(DSA_KV, 0), row(DSA_QI, 0), row(128, 0)],
        out_shape=[jax.ShapeDtypeStruct((R, DSA_Q), BF16), jax.ShapeDtypeStruct((R, DSA_KV), F32),
                   jax.ShapeDtypeStruct((R, DSA_KV), BF16), jax.ShapeDtypeStruct((R, DSA_KV), BF16),
                   jax.ShapeDtypeStruct((R, DSA_QI), BF16), jax.ShapeDtypeStruct((R, 128), F32)],
        compiler_params=_cparams(("arbitrary",)),
        name="rope_prompt",
    )(p, p, p, p, tail, *tabs)


def _sortable(x):
    bits = pltpu.bitcast(x, I32)
    return bits ^ ((bits >> 31) & 0x7FFFFFFF)


def _kth_largest(count_ge, shape, k):
    ans = jnp.full(shape, INT_MIN, I32)
    ans = jnp.where(count_ge(jnp.zeros(shape, I32)) >= k, 0, ans)

    def body(it, ans):
        cand = ans | lax.shift_left(jnp.int32(1), 30 - it)
        return jnp.where(count_ge(cand) >= k, cand, ans)

    return lax.fori_loop(0, 31, body, ans)


def _dsa_kernel(q_ref, qi_ref, wq_ref, kw_ref, k_ref, v_ref, o_ref,
                kab_ref, qis_ref, qs_ref, wb_ref, key_ref, m_ref, l_ref, acc_ref):
    i = pl.program_id(1)
    nkc = (i * QB + QB + KC - 1) // KC

    @pl.when(i == 0)
    def _():
        for c in range(NKC):
            t = kw_ref[c * KC:(c + 1) * KC, :]
            lane = lax.broadcasted_iota(I32, t.shape, 1)
            ka = jnp.where(lane < D_IDX, t, 0.0)
            kab_ref[0, c * KC:(c + 1) * KC, :] = ka.astype(BF16)
            kab_ref[1, c * KC:(c + 1) * KC, :] = pltpu.roll(ka, 64, 1).astype(BF16)

    for p in range(DSA_QI // 128):
        qis_ref[p * QB:(p + 1) * QB, :] = qi_ref[:, p * 128:(p + 1) * 128]
    for n in range(N_KV):
        for g in range(N_HEADS // N_KV):
            h = n * (N_HEADS // N_KV) + g
            qs_ref[n, g * QB:(g + 1) * QB, :] = q_ref[:, h * 128:(h + 1) * 128]
    wq = wq_ref[...]
    for h in range(N_IDX_HEADS):
        wb_ref[h] = jnp.broadcast_to(wq[:, D_IDX + h:D_IDX + h + 1] * IDX_SCALE, (QB, KC))

    rowpos = i * QB + lax.broadcasted_iota(I32, (QB, KC), 0)
    nt = (((1,), (1,)), ((), ()))

    def idx_body(c, carry):
        off = pl.multiple_of(c * KC, KC)
        da = lax.dot_general(qis_ref[...], kab_ref[0, pl.ds(off, KC), :], nt, preferred_element_type=F32)
        db = lax.dot_general(qis_ref[...], kab_ref[1, pl.ds(off, KC), :], nt, preferred_element_type=F32)
        sc = jnp.zeros((QB, KC), F32)
        for p in range(DSA_QI // 128):
            sc = sc + jnp.maximum(da[p * QB:(p + 1) * QB], 0.0) * wb_ref[2 * p]
            sc = sc + jnp.maximum(db[p * QB:(p + 1) * QB], 0.0) * wb_ref[2 * p + 1]
        col = c * KC + lax.broadcasted_iota(I32, (QB, KC), 1)
        key_ref[c] = jnp.where(col <= rowpos, _sortable(sc), INT_MIN)
        return carry

    lax.fori_loop(0, nkc, idx_body, 0)

    def count_ge(cand):
        def body(c, acc):
            return acc + (key_ref[c] >= cand).astype(I32)
        acc = lax.fori_loop(0, nkc, body, jnp.zeros((QB, KC), I32))
        return jnp.sum(acc, axis=1, keepdims=True)

    thr = jnp.maximum(_kth_largest(count_ge, (QB, 1), TOPK), INT_MIN + 1)

    m_ref[...] = jnp.full(m_ref.shape, NEG_BIG, F32)
    l_ref[...] = jnp.zeros(l_ref.shape, F32)
    acc_ref[...] = jnp.zeros(acc_ref.shape, F32)

    def att_body(c, carry):
        off = pl.multiple_of(c * KC, KC)
        sel = key_ref[c] >= thr
        for n in range(N_KV):
            kc = k_ref[pl.ds(off, KC), n * 128:(n + 1) * 128]
            vc = v_ref[pl.ds(off, KC), n * 128:(n + 1) * 128]
            s = lax.dot_general(qs_ref[n], kc, nt, preferred_element_type=F32) * HEAD_DIM ** -0.5
            ps = []
            for g in range(N_HEADS // N_KV):
                h = n * (N_HEADS // N_KV) + g
                sg = jnp.where(sel, s[g * QB:(g + 1) * QB], NEG_BIG)
                m_old = m_ref[h]
                m_new = jnp.maximum(m_old, jnp.max(sg, axis=1, keepdims=True))
                pg = jnp.where(sel, jnp.exp(sg - m_new), 0.0)
                alpha = jnp.exp(m_old - m_new)
                l_ref[h] = alpha * l_ref[h] + jnp.sum(pg, axis=1, keepdims=True)
                acc_ref[h] = alpha * acc_ref[h]
                m_ref[h] = m_new
                ps.append(pg.astype(BF16))
            pv = jnp.dot(jnp.concatenate(ps, axis=0), vc, preferred_element_type=F32)
            for g in range(N_HEADS // N_KV):
                h = n * (N_HEADS // N_KV) + g
                acc_ref[h] += pv[g * QB:(g + 1) * QB]
        return carry

    lax.fori_loop(0, nkc, att_body, 0)

    for h in range(N_HEADS):
        o_ref[:, h * 128:(h + 1) * 128] = (acc_ref[h] / l_ref[h]).astype(o_ref.dtype)


def _dsa_prompt(q_bf, qi_bf, kiwi, k_bf, v_bf):
    nq = TP // QB
    qrow = lambda w: pl.BlockSpec((QB, w), lambda b, i: (b * nq + i, 0))
    seq = lambda w: pl.BlockSpec((TP, w), lambda b, i: (b, 0))
    return pl.pallas_call(
        _dsa_kernel,
        grid=(NB, nq),
        in_specs=[qrow(DSA_Q), qrow(DSA_QI), qrow(128), seq(128), seq(DSA_KV), seq(DSA_KV)],
        out_specs=qrow(DSA_Q),
        out_shape=jax.ShapeDtypeStruct((R, DSA_Q), BF16),
        scratch_shapes=[pltpu.VMEM((2, TP, 128), BF16),
                        pltpu.VMEM((DSA_QI // 128 * QB, 128), BF16),
                        pltpu.VMEM((N_KV, N_HEADS // N_KV * QB, 128), BF16),
                        pltpu.VMEM((N_IDX_HEADS, QB, KC), F32),
                        pltpu.VMEM((NKC, QB, KC), I32),
                        pltpu.VMEM((N_HEADS, QB, 1), F32),
                        pltpu.VMEM((N_HEADS, QB, 1), F32),
                        pltpu.VMEM((N_HEADS, QB, 128), F32)],
        compiler_params=_cparams(("arbitrary", "arbitrary")),
        name="dsa_prompt",
    )(q_bf, qi_bf, kiwi, kiwi, k_bf, v_bf)


PG = 8
NPAGES = PAST // PAGE
NSEL = TOPK + 16


def _sample_prep_kernel(q_ref, k_ref, qi_ref, t_ref, c128_ref, s128_ref, c64_ref, s64_ref,
                        qo_ref, ko_ref, qio_ref, kw_ref, sn_ref):
    c128, s128 = c128_ref[...], s128_ref[...]
    c64, s64 = c64_ref[...], s64_ref[...]
    for h in range(N_HEADS):
        sl = slice(h * 128, (h + 1) * 128)
        x = q_ref[:, sl]
        qo_ref[:, sl] = x * c128 + _rot128(x) * s128
    for h in range(N_KV):
        sl = slice(h * 128, (h + 1) * 128)
        x = k_ref[:, sl]
        ko_ref[:, sl] = x * c128 + _rot128(x) * s128
    t = t_ref[...]
    lane = lax.broadcasted_iota(I32, t.shape, 1)
    kw = jnp.where(lane < D_IDX, t * c64 + _rot64(t) * s64, t)
    kw_ref[...] = kw
    ka = jnp.where(lane < D_IDX, kw, 0.0).astype(BF16).astype(F32)
    kb = pltpu.roll(ka, 64, 1)
    sn = jnp.zeros((NS, 1), F32)
    for p in range(DSA_QI // 128):
        sl = slice(p * 128, (p + 1) * 128)
        x = qi_ref[:, sl]
        r = x * c64 + _rot64(x) * s64
        qio_ref[:, sl] = r
        rb = r.astype(BF16).astype(F32)
        da = jnp.sum(rb * ka, axis=1, keepdims=True)
        db = jnp.sum(rb * kb, axis=1, keepdims=True)
        sn = sn + jnp.maximum(da, 0.0) * (kw[:, D_IDX + 2 * p:D_IDX + 2 * p + 1] * IDX_SCALE)
        sn = sn + jnp.maximum(db, 0.0) * (kw[:, D_IDX + 2 * p + 1:D_IDX + 2 * p + 2] * IDX_SCALE)
    sn_ref[...] = jnp.broadcast_to(sn, (NS, 128))


def _sample_prep(p, tail):
    tabs = [t.reshape(1, 128) for t in _rope_tables(jnp.full((1,), PAST, I32))]
    sb = S0 // NS
    row = lambda w, j: pl.BlockSpec((NS, w), lambda i: (sb, j))
    full = lambda w: pl.BlockSpec((NS, w), lambda i: (0, 0))
    tab = pl.BlockSpec((1, 128), lambda i: (0, 0))
    return pl.pallas_call(
        _sample_prep_kernel,
        grid=(1,),
        in_specs=[row(DSA_Q, 0), row(DSA_KV, 4), row(DSA_QI, 3), row(128, 0), tab, tab, tab, tab],
        out_specs=[full(DSA_Q), full(DSA_KV), full(DSA_QI), full(128), full(128)],
        out_shape=[jax.ShapeDtypeStruct((NS, DSA_Q), F32), jax.ShapeDtypeStruct((NS, DSA_KV), F32),
                   jax.ShapeDtypeStruct((NS, DSA_QI), F32), jax.ShapeDtypeStruct((NS, 128), F32),
                   jax.ShapeDtypeStruct((NS, 128), F32)],
        compiler_params=_cparams(("arbitrary",)),
        name="sample_prep",
    )(p, p, p, tail, *tabs)


def _sample_scores_kernel(pt_ref, qi_ref, w_ref, *refs):
    del pt_ref
    page_refs, out_ref = refs[:PG], refs[PG]
    qm = qi_ref[...].astype(BF16)
    w = w_ref[...] * IDX_SCALE
    nt = (((1,), (1,)), ((), ()))
    for u in range(PG):
        dots = lax.dot_general(qm, page_refs[u][...].astype(BF16), nt, preferred_element_type=F32)
        out_ref[u:u + 1, :] = jnp.sum(jnp.maximum(dots, 0.0) * w, axis=0, keepdims=True)


def _sample_scores(qi_r, kiwi_r, cache_idx_k, page_table, layer):
    qi3 = qi_r.reshape(NS, N_IDX_HEADS, D_IDX)
    w3 = kiwi_r[:, D_IDX:D_IDX + N_IDX_HEADS].reshape(NS, N_IDX_HEADS, 1)

    def page_spec(u):
        return pl.BlockSpec((None, None, PAGE, D_IDX), lambda b, g, pt: (layer, pt[b, g * PG + u], 0, 0))

    grid_spec = pltpu.PrefetchScalarGridSpec(
        num_scalar_prefetch=1,
        grid=(NS, NPAGES // PG),
        in_specs=[pl.BlockSpec((None, N_IDX_HEADS, D_IDX), lambda b, g, pt: (b, 0, 0)),
                  pl.BlockSpec((None, N_IDX_HEADS, 1), lambda b, g, pt: (b, 0, 0))]
                 + [page_spec(u) for u in range(PG)],
        out_specs=pl.BlockSpec((None, PG, PAGE), lambda b, g, pt: (b, g, 0)),
    )
    return pl.pallas_call(
        _sample_scores_kernel,
        grid_spec=grid_spec,
        out_shape=jax.ShapeDtypeStruct((NS, NPAGES, PAGE), F32),
        compiler_params=_cparams(("arbitrary", "arbitrary")),
        name="sample_scores",
    )(page_table, qi3, w3, *([cache_idx_k] * PG))


def _sample_select_kernel(sc_ref, sn_ref, idx_ref, meta_ref, pos_ref):
    keys = _sortable(sc_ref[...])
    knew = _sortable(sn_ref[...])[:, 0:1]

    def count_ge(cand):
        c = jnp.sum((keys >= cand).astype(I32), axis=1, keepdims=True)
        return jnp.sum(c, axis=0, keepdims=True) + (knew >= cand).astype(I32)

    thr = _kth_largest(count_ge, (1, 1), TOPK)
    sel = keys >= thr
    sel_new = (knew >= thr).astype(I32)
    r = lax.broadcasted_iota(I32, (PAGE, PAGE), 0)
    c = lax.broadcasted_iota(I32, (PAGE, PAGE), 1)
    upper = (r <= c).astype(F32).astype(BF16)
    lower = (c < r).astype(F32).astype(BF16)
    cum = jnp.dot(sel.astype(F32).astype(BF16), upper, preferred_element_type=F32)
    off = jnp.dot(lower, cum.astype(BF16), preferred_element_type=F32)[:, PAGE - 1:PAGE]
    pos_ref[...] = jnp.where(sel, off + cum - 1.0, -1.0)
    nsel = jnp.sum(cum[:, PAGE - 1:PAGE], axis=0, keepdims=True)

    jio = lax.broadcasted_iota(I32, (TOPK, PAGE), 0).astype(F32)
    lane = lax.broadcasted_iota(I32, (TOPK, PAGE), 1)

    def body(p, acc):
        return acc + jnp.where(jio == pos_ref[pl.ds(p, 1), :], p * PAGE + lane, 0)

    acc = lax.fori_loop(0, NPAGES, body, jnp.zeros((TOPK, PAGE), I32))
    idx_ref[...] = jnp.sum(acc, axis=1, keepdims=True)
    ml = lax.broadcasted_iota(I32, (1, 128), 1)
    nsel_i = jnp.minimum(nsel, float(TOPK)).astype(I32)
    meta_ref[...] = jnp.where(ml == 0, nsel_i, jnp.where(ml == 1, sel_new, 0))


def _sample_select(scores, snew):
    return pl.pallas_call(
        _sample_select_kernel,
        grid=(NS,),
        in_specs=[pl.BlockSpec((None, NPAGES, PAGE), lambda b: (b, 0, 0)),
                  pl.BlockSpec((None, 1, 128), lambda b: (b, 0, 0))],
        out_specs=[pl.BlockSpec((None, TOPK, 1), lambda b: (b, 0, 0)),
                   pl.BlockSpec((None, 1, 128), lambda b: (b, 0, 0))],
        out_shape=[jax.ShapeDtypeStruct((NS, TOPK, 1), I32), jax.ShapeDtypeStruct((NS, 1, 128), I32)],
        scratch_shapes=[pltpu.VMEM((NPAGES, PAGE), F32)],
        compiler_params=_cparams(("arbitrary",)),
        name="sample_select",
    )(scores, snew.reshape(NS, 1, 128))


def _sample_attend_kernel(idx_ref, meta_ref, pt_ref, q_ref, kn_ref, vn_ref, ck_ref, cv_ref, o_ref,
                          kbuf, vbuf, sem, *, layer):
    b = pl.program_id(0)

    def row_copies(j, row):
        return (pltpu.make_async_copy(ck_ref.at[layer, pl.ds(row, 1), :], kbuf.at[pl.ds(j, 1), :], sem.at[0]),
                pltpu.make_async_copy(cv_ref.at[layer, pl.ds(row, 1), :], vbuf.at[pl.ds(j, 1), :], sem.at[1]))

    def issue(j, carry):
        ix = idx_ref[b * TOPK + j]
        row = pt_ref[b * NPAGES + lax.shift_right_logical(ix, 7)] * PAGE + (ix & (PAGE - 1))
        for cp in row_copies(j, row):
            cp.start()
        return carry

    lax.fori_loop(0, TOPK, issue, 0)
    pad = jnp.zeros((NSEL - TOPK - 1, DSA_KV), F32)
    kbuf[TOPK:TOPK + 1, :] = kn_ref[pl.ds(b, 1), :]
    vbuf[TOPK:TOPK + 1, :] = vn_ref[pl.ds(b, 1), :]
    kbuf[TOPK + 1:, :] = pad
    vbuf[TOPK + 1:, :] = pad

    def drain(j, carry):
        for cp in row_copies(j, 0):
            cp.wait()
        return carry

    lax.fori_loop(0, TOPK, drain, 0)

    jio = lax.broadcasted_iota(I32, (1, NSEL), 1)
    valid = (jio < meta_ref[2 * b]) | ((jio == TOPK) & (meta_ref[2 * b + 1] > 0))
    nt = (((1,), (1,)), ((), ()))
    hg = N_HEADS // N_KV
    for n in range(N_KV):
        qn = q_ref[n * hg:(n + 1) * hg, :].astype(BF16)
        kb = kbuf[:, n * 128:(n + 1) * 128].astype(BF16)
        vb = vbuf[:, n * 128:(n + 1) * 128].astype(BF16)
        s = lax.dot_general(qn, kb, nt, preferred_element_type=F32) * HEAD_DIM ** -0.5
        s = jnp.where(valid, s, NEG_BIG)
        e = jnp.where(valid, jnp.exp(s - jnp.max(s, axis=1, keepdims=True)), 0.0)
        pn = e / jnp.sum(e, axis=1, keepdims=True)
        o_ref[n * hg:(n + 1) * hg, :] = jnp.dot(pn.astype(BF16), vb, preferred_element_type=F32)


def _sample_attend(idx, meta, page_table, q_r, k_r, p, cache_k, cache_v, layer):
    nl = cache_k.shape[0]
    ck = cache_k.reshape(nl, -1, DSA_KV)
    cv = cache_v.reshape(nl, -1, DSA_KV)
    grid_spec = pltpu.PrefetchScalarGridSpec(
        num_scalar_prefetch=3,
        grid=(NS,),
        in_specs=[pl.BlockSpec((None, N_HEADS, HEAD_DIM), lambda b, *_: (b, 0, 0)),
                  pl.BlockSpec((NS, DSA_KV), lambda b, *_: (0, 0)),
                  pl.BlockSpec((NS, DSA_KV), lambda b, *_: (S0 // NS, 5)),
                  pl.BlockSpec(memory_space=pl.ANY),
                  pl.BlockSpec(memory_space=pl.ANY)],
        out_specs=pl.BlockSpec((None, N_HEADS, HEAD_DIM), lambda b, *_: (b, 0, 0)),
        scratch_shapes=[pltpu.VMEM((NSEL, DSA_KV), F32), pltpu.VMEM((NSEL, DSA_KV), F32),
                        pltpu.SemaphoreType.DMA((2,))],
    )
    return pl.pallas_call(
        functools.partial(_sample_attend_kernel, layer=layer),
        grid_spec=grid_spec,
        out_shape=jax.ShapeDtypeStruct((NS, N_HEADS, HEAD_DIM), F32),
        compiler_params=_cparams(("arbitrary",)),
        name="sample_attend",
    )(idx.reshape(NS * TOPK), meta[:, 0, :2].reshape(NS * 2), page_table.reshape(NS * NPAGES),
      q_r.reshape(NS, N_HEADS, HEAD_DIM), k_r, p, ck, cv)


def _mixer_b(x, cache_k, cache_v, cache_idx_k, page_table, w_in, layer):
    p = _project(x, w_in, layer, DSA_MAIN, 1024)
    tail = _project_tail(x, w_in, layer, DSA_MAIN)
    q_bf, k_f, k_bf, v_bf, qi_bf, kiwi = _rope_prompt(p, tail)
    o = _dsa_prompt(q_bf, qi_bf, kiwi, k_bf, v_bf)
    q_r, k_r, qi_r, kiwi_r, snew = _sample_prep(p, tail)
    scores = _sample_scores(qi_r, kiwi_r, cache_idx_k, page_table, layer)
    idx, meta = _sample_select(scores, snew)
    o_s = _sample_attend(idx, meta, page_table, q_r, k_r, p, cache_k, cache_v, layer)
    o = lax.dynamic_update_slice(o, o_s.reshape(NS, DSA_Q).astype(BF16), (S0, 0))
    p3 = p.reshape(NB, TP, DSA_MAIN)
    outs = dict(
        k_p=k_f.reshape(NB, TP, N_KV, HEAD_DIM)[:, :T_REAL],
        v_p=p3[:, :T_REAL, DSA_Q + DSA_KV:DSA_Q + 2 * DSA_KV].reshape(NB, T_REAL, N_KV, HEAD_DIM),
        ki_p=kiwi.reshape(NB, TP, 128)[:, :T_REAL, :D_IDX],
        k_s=k_r.reshape(NS, 1, N_KV, HEAD_DIM),
        v_s=p[S0:S0 + NS, DSA_Q + DSA_KV:DSA_Q + 2 * DSA_KV].reshape(NS, 1, N_KV, HEAD_DIM),
        ki_s=kiwi_r[:, :D_IDX].reshape(NS, 1, D_IDX))
    return o, outs


DN_MAIN = DN_CONV_DIM + DN_VW
NEG_INF = float("-inf")


def _silu(x):
    return x * jax.nn.sigmoid(x)


def _softplus(x):
    return jnp.maximum(x, 0.0) + jnp.log1p(jnp.exp(-jnp.abs(x)))


def _conv_c_kernel(x_ref, w_ref, o_ref):
    c = pl.program_id(1)
    x = x_ref[...]
    row = lax.broadcasted_iota(I32, x.shape, 0)
    w = w_ref[...]
    y = w[3:4] * x
    for j in range(1, 4):
        y = y + w[3 - j:4 - j] * jnp.where(row >= j, pltpu.roll(x, j, 0), 0.0)
    s = _silu(y)
    nrm = s * lax.rsqrt(jnp.sum(jnp.square(s), -1, keepdims=True) + RMS_EPS)
    is_q = c < DN_QK_HEADS
    is_qk = c < 2 * DN_QK_HEADS
    o_ref[...] = jnp.where(is_qk, nrm * jnp.where(is_q, DN_DK ** -0.5, 1.0), s)


def _conv_c_prompt(p, conv_w, layer):
    p3 = p.reshape(NB, TP, DN_MAIN)
    return pl.pallas_call(
        _conv_c_kernel,
        grid=(NB, DN_CONV_DIM // 128),
        in_specs=[pl.BlockSpec((None, TP, 128), lambda b, c: (b, 0, c)),
                  pl.BlockSpec((None, 4, 128), lambda b, c: (layer, 0, c))],
        out_specs=pl.BlockSpec((None, TP, 128), lambda b, c: (b, 0, c)),
        out_shape=jax.ShapeDtypeStruct((NB, TP, DN_CONV_DIM), F32),
        compiler_params=_cparams(("arbitrary", "arbitrary")),
        name="conv_c_prompt",
    )(p3, conv_w)


def _gate_lanes(a_log, dt_bias, layer):
    pad = lambda v: jnp.pad(v[layer], (DN_V_HEADS, 128 - 2 * DN_V_HEADS)).reshape(1, 128)
    return pad(a_log), pad(dt_bias)


def _gates_kernel(t_ref, al_ref, dt_ref, o_ref):
    i = pl.program_id(0)
    t = t_ref[...]
    row = lax.broadcasted_iota(I32, t.shape, 0)
    lane = lax.broadcasted_iota(I32, t.shape, 1)
    real = (i % (TP // TR)) * TR + row < T_REAL
    beta = jnp.where(real, jax.nn.sigmoid(t), 0.0)
    g = jnp.where(real, -jnp.exp(al_ref[...]) * _softplus(t + dt_ref[...]), 0.0)
    pos = row & (CH - 1)
    sh = 1
    while sh < CH:
        g = g + jnp.where(pos >= sh, pltpu.roll(g, sh, 0), 0.0)
        sh *= 2
    o_ref[...] = jnp.where(lane < DN_V_HEADS, beta, g)


def _gates(tail, a_log, dt_bias, layer):
    al, dt = _gate_lanes(a_log, dt_bias, layer)
    vec = pl.BlockSpec((1, 128), lambda i: (0, 0))
    return pl.pallas_call(
        _gates_kernel,
        grid=(R // TR,),
        in_specs=[pl.BlockSpec((TR, 128), lambda i: (i, 0)), vec, vec],
        out_specs=pl.BlockSpec((TR, 128), lambda i: (i, 0)),
        out_shape=jax.ShapeDtypeStruct((R, 128), F32),
        compiler_params=_cparams(("arbitrary",)),
        name="dn_gates",
    )(tail, al, dt)


def _delta_kernel(q_ref, k_ref, v_ref, z_ref, g_ref, gr_ref, nw_ref, o_ref, s_ref):
    c = pl.program_id(1)

    @pl.when(c == 0)
    def _():
        s_ref[...] = jnp.zeros(s_ref.shape, F32)

    rows = HG * CH
    nt = (((1,), (1,)), ((), ()))
    tn = (((0,), (0,)), ((), ()))
    gates = g_ref[...]
    nw = nw_ref[...]
    ri = lax.broadcasted_iota(I32, (rows, rows), 0)
    ci = lax.broadcasted_iota(I32, (rows, rows), 1)
    same = (ri // CH) == (ci // CH)
    causal = same & (ci <= ri)
    strict = same & (ci < ri)
    rb = lax.broadcasted_iota(I32, (rows, HG * DN_DV), 0) // CH
    cb = lax.broadcasted_iota(I32, (rows, HG * DN_DV), 1) // DN_DV
    diag = rb == cb

    for g in range(NG):
        heads = [HG * g + hh for hh in range(HG)]
        stack = lambda ref, idx: jnp.concatenate([ref[:, j * 128:(j + 1) * 128] for j in idx], axis=0)
        kst = stack(k_ref, [h // 2 for h in heads])
        qst = stack(q_ref, [h // 2 for h in heads])
        vst = stack(v_ref, heads)
        beta = jnp.concatenate([gates[:, h:h + 1] for h in heads], axis=0)
        gc = jnp.concatenate([gates[:, DN_V_HEADS + h:DN_V_HEADS + h + 1] for h in heads], axis=0)
        gl = jnp.concatenate([jnp.broadcast_to(gates[CH - 1:CH, DN_V_HEADS + h:DN_V_HEADS + h + 1], (CH, 1))
                              for h in heads], axis=0)
        gcrow = gr_ref[g:g + 1, :]
        glrow = jnp.concatenate([jnp.broadcast_to(gcrow[:, hh * CH + CH - 1:hh * CH + CH], (1, DN_DV))
                                 for hh in range(HG)], axis=1)

        kb = kst.astype(BF16)
        kk = lax.dot_general(kb, kb, nt, preferred_element_type=F32)
        qk = lax.dot_general(qst.astype(BF16), kb, nt, preferred_element_type=F32)
        decay = jnp.where(causal, jnp.exp(jnp.where(causal, gc - gcrow, 0.0)), 0.0)
        a = jnp.where(strict, beta * decay * kk, 0.0)
        pmat = (qk * decay).astype(BF16)
        eg = jnp.exp(gc)
        x = jnp.concatenate([beta * vst, (beta * eg) * kst], axis=1)
        pw = a.astype(BF16)
        x = x - jnp.dot(pw, x.astype(BF16), preferred_element_type=F32)
        for _ in range(5):
            p2 = jnp.dot(pw, pw, preferred_element_type=F32)
            pw = p2.astype(BF16)
            x = x + jnp.dot(pw, x.astype(BF16), preferred_element_type=F32)
        u, w = x[:, :DN_DV], x[:, DN_DV:]

        s = s_ref[g]
        sb = s.astype(BF16)
        ws = jnp.dot(w.astype(BF16), sb, preferred_element_type=F32)
        u4 = jnp.concatenate([u] * HG, axis=1)
        delta = jnp.where(diag, u4 - ws, 0.0).astype(BF16)
        qs = jnp.dot((qst * eg).astype(BF16), sb, preferred_element_type=F32)
        obd = jnp.where(diag, qs, 0.0) + jnp.dot(pmat, delta, preferred_element_type=F32)
        ost = obd[:, :DN_DV]
        for hh in range(1, HG):
            ost = ost + obd[:, hh * DN_DV:(hh + 1) * DN_DV]
        kd = (kst * jnp.exp(gl - gc)).astype(BF16)
        s_ref[g] = jnp.exp(glrow) * s + lax.dot_general(kd, delta, tn, preferred_element_type=F32)

        for hh, h in enumerate(heads):
            oh = ost[hh * CH:(hh + 1) * CH]
            oh = oh * lax.rsqrt(jnp.mean(jnp.square(oh), -1, keepdims=True) + RMS_EPS) * nw
            o_ref[:, h * 128:(h + 1) * 128] = (oh * _silu(z_ref[:, h * 128:(h + 1) * 128])).astype(o_ref.dtype)


def _delta_prompt(qkvc, p, gates, norm_w, layer):
    p3 = p.reshape(NB, TP, DN_MAIN)
    g3 = gates.reshape(NB, TP, 128)
    gcrow = g3[:, :, DN_V_HEADS:2 * DN_V_HEADS].reshape(NB, NCH, CH, DN_V_HEADS)
    gcrow = jnp.swapaxes(gcrow, 2, 3).reshape(NB, NCH, NG, HG * CH)
    blk = lambda w, j: pl.BlockSpec((None, CH, w), lambda b, c: (b, c, j))
    o, s = pl.pallas_call(
        _delta_kernel,
        grid=(NB, NCH),
        in_specs=[blk(DN_QK, 0), blk(DN_QK, 1), blk(DN_VW, 1), blk(DN_VW, 2), blk(128, 0),
                  pl.BlockSpec((None, None, NG, HG * CH), lambda b, c: (b, c, 0, 0)),
                  pl.BlockSpec((None, 1, DN_DV), lambda b, c: (layer, 0, 0))],
        out_specs=[blk(DN_VW, 0),
                   pl.BlockSpec((None, NG, DN_DK, HG * DN_DV), lambda b, c: (b, 0, 0, 0))],
        out_shape=[jax.ShapeDtypeStruct((NB, TP, DN_VW), BF16),
                   jax.ShapeDtypeStruct((NB, NG, DN_DK, HG * DN_DV), F32)],
        compiler_params=_cparams(("arbitrary", "arbitrary")),
        name="delta_prompt",
    )(qkvc, qkvc, qkvc, p3, g3, gcrow, norm_w.reshape(-1, 1, DN_DV))
    s = s.reshape(NB, NG, DN_DK, HG, DN_DV)
    s = jnp.transpose(s, (0, 1, 3, 2, 4)).reshape(NB, DN_V_HEADS, DN_DK, DN_DV)
    return o.reshape(R, DN_VW), s


def _delta_sample_kernel(xq_ref, xk_ref, xv_ref, z_ref, sq_ref, sk_ref, sv_ref, wq_ref, wk_ref, wv_ref,
                         t_ref, al_ref, dt_ref, nw_ref, s_ref, o_ref, so_ref):
    hv = pl.program_id(0)

    def conv(st_ref, x_ref, w_ref):
        w = w_ref[...]
        y = w[0:1] * st_ref[0] + w[1:2] * st_ref[1] + w[2:3] * st_ref[2] + w[3:4] * x_ref[...]
        return _silu(y)

    q = conv(sq_ref, xq_ref, wq_ref)
    k = conv(sk_ref, xk_ref, wk_ref)
    v = conv(sv_ref, xv_ref, wv_ref)
    q = q * lax.rsqrt(jnp.sum(jnp.square(q), -1, keepdims=True) + RMS_EPS) * DN_DK ** -0.5
    k = k * lax.rsqrt(jnp.sum(jnp.square(k), -1, keepdims=True) + RMS_EPS)
    t = t_ref[...]
    lane = lax.broadcasted_iota(I32, t.shape, 1)
    beta_all = jax.nn.sigmoid(t)
    g_all = -jnp.exp(al_ref[...]) * _softplus(t + dt_ref[...])
    beta = jnp.sum(jnp.where(lane == hv, beta_all, 0.0), axis=1, keepdims=True)
    decay = jnp.exp(jnp.sum(jnp.where(lane == DN_V_HEADS + hv, g_all, 0.0), axis=1, keepdims=True))
    eye = lax.broadcasted_iota(I32, (DN_DK, DN_DK), 0) == lax.broadcasted_iota(I32, (DN_DK, DN_DK), 1)
    to_col = lambda r: jnp.sum(jnp.where(eye, r, 0.0), axis=1, keepdims=True)
    nw = nw_ref[...]
    z = z_ref[...]
    for b in range(NS):
        s = s_ref[b] * decay[b:b + 1]
        kcol = to_col(k[b:b + 1])
        v_old = jnp.sum(s * kcol, axis=0, keepdims=True)
        s = s + kcol * (beta[b:b + 1] * (v[b:b + 1] - v_old))
        so_ref[b] = s
        o = jnp.sum(s * to_col(q[b:b + 1]), axis=0, keepdims=True)
        o = o * lax.rsqrt(jnp.mean(jnp.square(o), -1, keepdims=True) + RMS_EPS) * nw
        o_ref[b:b + 1, :] = o * _silu(z[b:b + 1])


def _delta_sample(p, tail, conv_state, state, conv_w, a_log, dt_bias, norm_w, layer):
    st = jnp.swapaxes(conv_state[layer], 0, 1)
    al, dt = _gate_lanes(a_log, dt_bias, layer)
    sb = S0 // NS
    nq, nqk = DN_QK_HEADS, 2 * DN_QK_HEADS
    xrow = lambda f: pl.BlockSpec((NS, 128), lambda h: (sb, f(h)))
    srow = lambda f: pl.BlockSpec((3, NS, 128), lambda h: (0, 0, f(h)))
    wrow = lambda f: pl.BlockSpec((None, 4, 128), lambda h: (layer, 0, f(h)))
    fq, fk, fv = (lambda h: h // 2), (lambda h: nq + h // 2), (lambda h: nqk + h)
    vec = pl.BlockSpec((1, 128), lambda h: (0, 0))
    sspec = pl.BlockSpec((None, NS, None, DN_DK, DN_DV), lambda h: (layer, 0, h, 0, 0))
    return pl.pallas_call(
        _delta_sample_kernel,
        grid=(DN_V_HEADS,),
        in_specs=[xrow(fq), xrow(fk), xrow(fv), xrow(lambda h: nqk + DN_V_HEADS + h),
                  srow(fq), srow(fk), srow(fv), wrow(fq), wrow(fk), wrow(fv),
                  pl.BlockSpec((NS, 128), lambda h: (sb, 0)), vec, vec,
                  pl.BlockSpec((None, 1, DN_DV), lambda h: (layer, 0, 0)), sspec],
        out_specs=[pl.BlockSpec((NS, 128), lambda h: (0, h)),
                   pl.BlockSpec((NS, None, DN_DK, DN_DV), lambda h: (0, h, 0, 0))],
        out_shape=[jax.ShapeDtypeStruct((NS, DN_VW), F32),
                   jax.ShapeDtypeStruct((NS, DN_V_HEADS, DN_DK, DN_DV), F32)],
        compiler_params=_cparams(("arbitrary",)),
        name="delta_sample",
    )(p, p, p, p, st, st, st, conv_w, conv_w, conv_w, tail, al, dt, norm_w.reshape(-1, 1, DN_DV), state)


def _mixer_c(x, conv_state, state, w_in, conv_w, a_log, dt_bias, norm_w, layer):
    p = _project(x, w_in, layer, DN_MAIN, 1024)
    tail = _project_tail(x, w_in, layer, DN_MAIN)
    qkvc = _conv_c_prompt(p, conv_w, layer)
    gates = _gates(tail, a_log, dt_bias, layer)
    o, s_p = _delta_prompt(qkvc, p, gates, norm_w, layer)
    o_s, s_s = _delta_sample(p, tail, conv_state, state, conv_w, a_log, dt_bias, norm_w, layer)
    o = lax.dynamic_update_slice(o, o_s.astype(BF16), (S0, 0))
    p3 = p.reshape(NB, TP, DN_MAIN)
    outs = dict(
        conv_p=p3[:, T_REAL - 3:T_REAL, :DN_CONV_DIM],
        conv_s=jnp.concatenate([conv_state[layer][:, 1:], p[S0:S0 + NS, None, :DN_CONV_DIM]], axis=1),
        s_p=s_p, s_s=s_s)
    return o, outs


def _build_x(x_prompt, x_sample, meta_tokens):
    meta = jnp.broadcast_to(meta_tokens[None], (NB, N_META, D))
    xp = jnp.concatenate([meta, x_prompt, jnp.zeros((NB, TP - T_REAL, D), F32)], axis=1)
    x = xp.reshape(R, D)
    return lax.dynamic_update_slice(x, x_sample.reshape(NS, D), (S0, 0))


def kernel(x_prompt, x_sample, state_conv_a, cache_k, cache_v, cache_idx_k, state_conv_c, state_delta, page_table, meta_tokens, w_in_a, conv_w_a, w_out_a, w_in_b, w_out_b, w_in_c, conv_w_c, a_log_c, dt_bias_c, norm_w_c, w_out_c, ln1_g, ln1_b, w1, w2, ln2_g, ln2_b):
    x = _build_x(x_prompt, x_sample, meta_tokens)
    conv_a_p, conv_a_s = [], []
    b_outs, c_outs = [], []
    for i in range(DEPTH):
        j, kind = i // 3, i % 3
        if kind == 0:
            y, st_p, st_s = _mixer_a(x, state_conv_a[j], w_in_a, conv_w_a, j)
            conv_a_p.append(st_p)
            conv_a_s.append(st_s)
            w_out = w_out_a
        elif kind == 1:
            y, outs = _mixer_b(x, cache_k, cache_v, cache_idx_k, page_table, w_in_b, j)
            b_outs.append(outs)
            w_out = w_out_b
        else:
            y, outs = _mixer_c(x, state_conv_c, state_delta, w_in_c, conv_w_c, a_log_c, dt_bias_c,
                               norm_w_c, j)
            c_outs.append(outs)
            w_out = w_out_c
        x = _out_ln(y, w_out, j, x, ln1_g, ln1_b, i)
        x = _mlp_ln(x, w1, w2, ln2_g, ln2_b, i)
    x3 = x.reshape(NB, TP, D)
    stack = lambda outs, name: jnp.stack([o[name] for o in outs])
    return (x3[:, N_META:T_REAL], x[S0:S0 + NS].reshape(NS, 1, D),
            jnp.stack(conv_a_p), jnp.stack(conv_a_s),
            stack(b_outs, "k_p"), stack(b_outs, "v_p"), stack(b_outs, "ki_p"),
            stack(b_outs, "k_s"), stack(b_outs, "v_s"), stack(b_outs, "ki_s"),
            stack(c_outs, "conv_p"), stack(c_outs, "conv_s"),
            stack(c_outs, "s_p"), stack(c_outs, "s_s"))
```

```python
import functools

import jax
import jax.numpy as jnp
from jax import lax
from jax.experimental import pallas as pl
from jax.experimental.pallas import tpu as pltpu

F32 = jnp.float32
BF16 = jnp.bfloat16
I32 = jnp.int32

D = 2048
NB = 2
SEQ = 4096
DEPTH = 4
NS = 8
PAST = 16384
PAGE = 128
N_META = 16
T_REAL = N_META + SEQ
TP = 4224
R = NB * TP
S0 = T_REAL
ALPHA = (2 * DEPTH) ** 0.25
LN_EPS = 1e-5
RMS_EPS = 1e-6
D_FF = 4 * D
TM = 768

HEAD_DIM = 128
N_HEADS = 16
N_KV = 4
N_IDX_HEADS = 16
D_IDX = 64
TOPK = 256
ROPE_THETA = 10000.0
QB = 128
KC = 384

DN_QK_HEADS = 16
DN_V_HEADS = 32
DN_DK = 128
DN_DV = 128
DN_QK = DN_QK_HEADS * DN_DK
DN_VW = DN_V_HEADS * DN_DV
DN_CONV_DIM = 2 * DN_QK + DN_VW
CH = 64
NCH = TP // CH
HG = 4
NG = DN_V_HEADS // HG

VMEM_LIMIT = 58 * 1024 * 1024


def _cparams(sem):
    return pltpu.CompilerParams(dimension_semantics=sem, vmem_limit_bytes=VMEM_LIMIT)


def _layer_norm(y, g, b):
    mu = jnp.mean(y, -1, keepdims=True)
    yc = y - mu
    var = jnp.mean(jnp.square(yc), -1, keepdims=True)
    return yc * lax.rsqrt(var + LN_EPS) * g + b


def _mm_kernel(x_ref, w_ref, o_ref, wbf_ref):
    @pl.when(pl.program_id(1) == 0)
    def _():
        wbf_ref[...] = w_ref[...].astype(BF16)

    o_ref[...] = jnp.dot(x_ref[...].astype(BF16), wbf_ref[...],
                         preferred_element_type=F32).astype(o_ref.dtype)


def _project(x, w, layer, ncols, tn):
    rows, k = x.shape
    return pl.pallas_call(
        _mm_kernel,
        grid=(ncols // tn, rows // TM),
        in_specs=[pl.BlockSpec((TM, k), lambda j, i: (i, 0)),
                  pl.BlockSpec((None, k, tn), lambda j, i: (layer, 0, j))],
        out_specs=pl.BlockSpec((TM, tn), lambda j, i: (i, j)),
        out_shape=jax.ShapeDtypeStruct((rows, ncols), F32),
        scratch_shapes=[pltpu.VMEM((k, tn), BF16)],
        compiler_params=_cparams(("arbitrary", "arbitrary")),
        name="project",
    )(x, w)


def _mm_t_kernel(x_ref, w_ref, o_ref, wbf_ref, *, valid):
    @pl.when(pl.program_id(1) == 0)
    def _():
        w = w_ref[...]
        if valid < w.shape[0]:
            w = jnp.where(lax.broadcasted_iota(I32, w.shape, 0) < valid, w, 0.0)
        wbf_ref[...] = w.T.astype(BF16)

    o_ref[...] = jnp.dot(x_ref[...].astype(BF16), wbf_ref[...], preferred_element_type=F32)


def _project_t(x, wt, layer, col0, ncols, tn):
    rows, k = x.shape
    valid = min(tn, wt.shape[1] - col0 - (ncols // tn - 1) * tn)
    j0 = col0 // tn
    return pl.pallas_call(
        functools.partial(_mm_t_kernel, valid=valid),
        grid=(ncols // tn, rows // TM),
        in_specs=[pl.BlockSpec((TM, k), lambda j, i: (i, 0)),
                  pl.BlockSpec((None, tn, k), lambda j, i: (layer, j0 + j, 0))],
        out_specs=pl.BlockSpec((TM, tn), lambda j, i: (i, j)),
        out_shape=jax.ShapeDtypeStruct((rows, ncols), F32),
        scratch_shapes=[pltpu.VMEM((k, tn), BF16)],
        compiler_params=_cparams(("arbitrary", "arbitrary")),
        name="project_t",
    )(x, wt)


def _out_ln_kernel(a_ref, w_ref, r_ref, g_ref, b_ref, o_ref, *, nk):
    k = pl.program_id(1)
    part = jnp.dot(a_ref[...], w_ref[...].astype(BF16), preferred_element_type=F32)

    @pl.when(k == 0)
    def _():
        o_ref[...] = part

    @pl.when(k > 0)
    def _():
        o_ref[...] += part

    @pl.when(k == nk - 1)
    def _():
        y = ALPHA * r_ref[...] + o_ref[...]
        o_ref[...] = _layer_norm(y, g_ref[...], b_ref[...])


def _out_ln(a, w, wl, resid, g, b, gl):
    rows, k = a.shape
    tk = 512
    nk = k // tk
    return pl.pallas_call(
        functools.partial(_out_ln_kernel, nk=nk),
        grid=(rows // TM, nk),
        in_specs=[pl.BlockSpec((TM, tk), lambda i, kk: (i, kk)),
                  pl.BlockSpec((None, tk, D), lambda i, kk: (wl, kk, 0)),
                  pl.BlockSpec((TM, D), lambda i, kk: (i, 0)),
                  pl.BlockSpec((None, 1, D), lambda i, kk: (gl, 0, 0)),
                  pl.BlockSpec((None, 1, D), lambda i, kk: (gl, 0, 0))],
        out_specs=pl.BlockSpec((TM, D), lambda i, kk: (i, 0)),
        out_shape=jax.ShapeDtypeStruct((rows, D), F32),
        compiler_params=_cparams(("arbitrary", "arbitrary")),
        name="out_ln",
    )(a, w, resid, g.reshape(DEPTH, 1, D), b.reshape(DEPTH, 1, D))


def _mlp_kernel(x_ref, w1_ref, w2_ref, g_ref, b_ref, o_ref, xbf_ref, *, nf):
    f = pl.program_id(1)

    @pl.when(f == 0)
    def _():
        xbf_ref[...] = x_ref[...].astype(BF16)

    h = jnp.dot(xbf_ref[...], w1_ref[...].astype(BF16), preferred_element_type=F32)
    h = jnp.square(jnp.maximum(h, 0.0)).astype(BF16)
    part = jnp.dot(h, w2_ref[...].astype(BF16), preferred_element_type=F32)

    @pl.when(f == 0)
    def _():
        o_ref[...] = part

    @pl.when(f > 0)
    def _():
        o_ref[...] += part

    @pl.when(f == nf - 1)
    def _():
        y = ALPHA * x_ref[...] + o_ref[...]
        o_ref[...] = _layer_norm(y, g_ref[...], b_ref[...])


def _mlp_ln(x, w1, w2, g, b, layer):
    rows = x.shape[0]
    tf = 512
    nf = D_FF // tf
    return pl.pallas_call(
        functools.partial(_mlp_kernel, nf=nf),
        grid=(rows // TM, nf),
        in_specs=[pl.BlockSpec((TM, D), lambda i, f: (i, 0)),
                  pl.BlockSpec((None, D, tf), lambda i, f: (layer, 0, f)),
                  pl.BlockSpec((None, tf, D), lambda i, f: (layer, f, 0)),
                  pl.BlockSpec((None, 1, D), lambda i, f: (layer, 0, 0)),
                  pl.BlockSpec((None, 1, D), lambda i, f: (layer, 0, 0))],
        out_specs=pl.BlockSpec((TM, D), lambda i, f: (i, 0)),
        out_shape=jax.ShapeDtypeStruct((rows, D), F32),
        scratch_shapes=[pltpu.VMEM((TM, D), BF16)],
        compiler_params=_cparams(("arbitrary", "arbitrary")),
        name="mlp_ln",
    )(x, w1, w2, g.reshape(DEPTH, 1, D), b.reshape(DEPTH, 1, D))


CB_A = 128


def _conv_a_kernel(bg_ref, cg_ref, h_ref, w_ref, y_ref, st_ref):
    u = cg_ref[...] * h_ref[...]
    row = lax.broadcasted_iota(I32, u.shape, 0)
    u1 = jnp.where(row >= 1, pltpu.roll(u, 1, 0), 0.0)
    u2 = jnp.where(row >= 2, pltpu.roll(u, 2, 0), 0.0)
    w = w_ref[...]
    y = bg_ref[...] * (w[0:1] * u2 + w[1:2] * u1 + w[2:3] * u)
    y_ref[...] = y.astype(BF16)
    st_ref[...] = u[T_REAL - 8:T_REAL]


def _conv_a_prompt(p, conv_w, layer):
    p3 = p.reshape(NB, TP, 3 * D)
    nc = D // CB_A
    y, st = pl.pallas_call(
        _conv_a_kernel,
        grid=(NB, nc),
        in_specs=[pl.BlockSpec((None, TP, CB_A), lambda b, c: (b, 0, c)),
                  pl.BlockSpec((None, TP, CB_A), lambda b, c: (b, 0, nc + c)),
                  pl.BlockSpec((None, TP, CB_A), lambda b, c: (b, 0, 2 * nc + c)),
                  pl.BlockSpec((None, 3, CB_A), lambda b, c: (layer, 0, c))],
        out_specs=[pl.BlockSpec((None, TP, CB_A), lambda b, c: (b, 0, c)),
                   pl.BlockSpec((None, 8, CB_A), lambda b, c: (b, 0, c))],
        out_shape=[jax.ShapeDtypeStruct((NB, TP, D), BF16),
                   jax.ShapeDtypeStruct((NB, 8, D), F32)],
        compiler_params=_cparams(("arbitrary", "arbitrary")),
        name="conv_a_prompt",
    )(p3, p3, p3, conv_w)
    return y.reshape(R, D), st[:, 6:8]


def _conv_a_sample_kernel(p_ref, st_ref, w_ref, yin_ref, y_ref, nst_ref):
    del yin_ref
    p = p_ref[...]
    u = p[:, D:2 * D] * p[:, 2 * D:]
    w = w_ref[...]
    y = p[:, :D] * (w[0:1] * st_ref[0] + w[1:2] * st_ref[1] + w[2:3] * u)
    y_ref[...] = jnp.concatenate([y, jnp.zeros_like(y)], axis=0).astype(BF16)
    nst_ref[0] = st_ref[1]
    nst_ref[1] = u


def _conv_a_sample(p, state, conv_w, layer, y):
    st_t = jnp.swapaxes(state, 0, 1)
    y, nst = pl.pallas_call(
        _conv_a_sample_kernel,
        grid=(1,),
        in_specs=[pl.BlockSpec((NS, 3 * D), lambda i: (S0 // NS, 0)),
                  pl.BlockSpec((2, NS, D), lambda i: (0, 0, 0)),
                  pl.BlockSpec((None, 3, D), lambda i: (layer, 0, 0)),
                  pl.BlockSpec(memory_space=pl.ANY)],
        out_specs=[pl.BlockSpec((16, D), lambda i: (S0 // 16, 0)),
                   pl.BlockSpec((2, NS, D), lambda i: (0, 0, 0))],
        out_shape=[jax.ShapeDtypeStruct((R, D), BF16),
                   jax.ShapeDtypeStruct((2, NS, D), F32)],
        input_output_aliases={3: 0},
        compiler_params=_cparams(("arbitrary",)),
        name="conv_a_sample",
    )(p, st_t, conv_w, y)
    return y, jnp.swapaxes(nst, 0, 1)


def _mixer_a(x, state, w_in, conv_w, layer):
    p = _project(x, w_in, layer, 3 * D, 1024)
    y, st_p = _conv_a_prompt(p, conv_w, layer)
    y, st_s = _conv_a_sample(p, state, conv_w, layer, y)
    return y, st_p, st_s


DSA_Q = N_HEADS * HEAD_DIM
DSA_KV = N_KV * HEAD_DIM
DSA_QI = N_IDX_HEADS * D_IDX
DSA_MAIN = DSA_Q + 2 * DSA_KV + DSA_QI
TR = 384
NKC = TP // KC
INT_MIN = -2 ** 31
IDX_SCALE = D_IDX ** -0.5 * N_IDX_HEADS ** -0.5
NEG_BIG = -1e30


def _rope_tables(pos):
    def tab(d):
        inv = ROPE_THETA ** (-jnp.arange(0, d, 2, dtype=F32) / d)
        ang = pos.astype(F32)[:, None] * inv[None, :]
        c, s = jnp.cos(ang), jnp.sin(ang)
        return jnp.concatenate([c, c], -1), jnp.concatenate([-s, s], -1)
    c128, s128 = tab(HEAD_DIM)
    c64, s64 = tab(D_IDX)
    return c128, s128, jnp.concatenate([c64, c64], -1), jnp.concatenate([s64, s64], -1)


def _rot128(x):
    return pltpu.roll(x, 64, 1)


def _rot64(x):
    lane = lax.broadcasted_iota(I32, x.shape, 1)
    return jnp.where((lane & 32) == 0, pltpu.roll(x, 96, 1), pltpu.roll(x, 32, 1))


def _rope_kernel(q_ref, k_ref, v_ref, qi_ref, t_ref, c128_ref, s128_ref, c64_ref, s64_ref,
                 qo_ref, kf_ref, kb_ref, vb_ref, qio_ref, kw_ref):
    c128, s128 = c128_ref[...], s128_ref[...]
    c64, s64 = c64_ref[...], s64_ref[...]
    for h in range(N_HEADS):
        sl = slice(h * 128, (h + 1) * 128)
        x = q_ref[:, sl]
        qo_ref[:, sl] = (x * c128 + _rot128(x) * s128).astype(qo_ref.dtype)
    for h in range(N_KV):
        sl = slice(h * 128, (h + 1) * 128)
        x = k_ref[:, sl]
        r = x * c128 + _rot128(x) * s128
        kf_ref[:, sl] = r
        kb_ref[:, sl] = r.astype(BF16)
    vb_ref[...] = v_ref[...].astype(BF16)
    for p in range(DSA_QI // 128):
        sl = slice(p * 128, (p + 1) * 128)
        x = qi_ref[:, sl]
        qio_ref[:, sl] = (x * c64 + _rot64(x) * s64).astype(qio_ref.dtype)
    t = t_ref[...]
    lane = lax.broadcasted_iota(I32, t.shape, 1)
    kw_ref[...] = jnp.where(lane < D_IDX, t * c64 + _rot64(t) * s64, t)


def _rope_prompt(p, tail):
    pos = jnp.arange(TP, dtype=I32)
    tabs = _rope_tables(pos)
    nt = TP // TR
    tab_spec = pl.BlockSpec((TR, 128), lambda i: (i % nt, 0))
    row = lambda w, j: pl.BlockSpec((TR, w), lambda i: (i, j))
    return pl.pallas_call(
        _rope_kernel,
        grid=(R // TR,),
        in_specs=[row(DSA_Q, 0), row(DSA_KV, 4), row(DSA_KV, 5), row(DSA_QI, 3), row(128, 0),
                  tab_spec, tab_spec, tab_spec, tab_spec],
        out_specs=[row(DSA_Q, 0), row(DSA_KV, 0), row(DSA_KV, 0), row(DSA_KV, 0), row(DSA_QI, 0), row(128, 0)],
        out_shape=[jax.ShapeDtypeStruct((R, DSA_Q), BF16), jax.ShapeDtypeStruct((R, DSA_KV), F32),
                   jax.ShapeDtypeStruct((R, DSA_KV), BF16), jax.ShapeDtypeStruct((R, DSA_KV), BF16),
                   jax.ShapeDtypeStruct((R, DSA_QI), BF16), jax.ShapeDtypeStruct((R, 128), F32)],
        compiler_params=_cparams(("arbitrary",)),
        name="rope_prompt",
    )(p, p, p, p, tail, *tabs)


def _sortable(x):
    bits = pltpu.bitcast(x, I32)
    return bits ^ ((bits >> 31) & 0x7FFFFFFF)


def _kth_largest(count_ge, shape, k):
    ans = jnp.full(shape, INT_MIN, I32)
    ans = jnp.where(count_ge(jnp.zeros(shape, I32)) >= k, 0, ans)

    def body(it, ans):
        cand = ans | lax.shift_left(jnp.int32(1), 30 - it)
        return jnp.where(count_ge(cand) >= k, cand, ans)

    return lax.fori_loop(0, 31, body, ans)


def _dsa_kernel(q_ref, qi_ref, wq_ref, kw_ref, k_ref, v_ref, o_ref,
                kab_ref, qis_ref, qs_ref, wb_ref, key_ref, m_ref, l_ref, acc_ref):
    i = pl.program_id(1)
    nkc = (i * QB + QB + KC - 1) // KC

    @pl.when(i == 0)
    def _():
        for c in range(NKC):
            t = kw_ref[c * KC:(c + 1) * KC, :]
            lane = lax.broadcasted_iota(I32, t.shape, 1)
            ka = jnp.where(lane < D_IDX, t, 0.0)
            kab_ref[0, c * KC:(c + 1) * KC, :] = ka.astype(BF16)
            kab_ref[1, c * KC:(c + 1) * KC, :] = pltpu.roll(ka, 64, 1).astype(BF16)

    for p in range(DSA_QI // 128):
        qis_ref[p * QB:(p + 1) * QB, :] = qi_ref[:, p * 128:(p + 1) * 128]
    for n in range(N_KV):
        for g in range(N_HEADS // N_KV):
            h = n * (N_HEADS // N_KV) + g
            qs_ref[n, g * QB:(g + 1) * QB, :] = q_ref[:, h * 128:(h + 1) * 128]
    wq = wq_ref[...]
    for h in range(N_IDX_HEADS):
        wb_ref[h] = jnp.broadcast_to(wq[:, D_IDX + h:D_IDX + h + 1] * IDX_SCALE, (QB, KC))

    rowpos = i * QB + lax.broadcasted_iota(I32, (QB, KC), 0)
    nt = (((1,), (1,)), ((), ()))

    def idx_body(c, carry):
        off = pl.multiple_of(c * KC, KC)
        da = lax.dot_general(qis_ref[...], kab_ref[0, pl.ds(off, KC), :], nt, preferred_element_type=F32)
        db = lax.dot_general(qis_ref[...], kab_ref[1, pl.ds(off, KC), :], nt, preferred_element_type=F32)
        sc = jnp.zeros((QB, KC), F32)
        for p in range(DSA_QI // 128):
            sc = sc + jnp.maximum(da[p * QB:(p + 1) * QB], 0.0) * wb_ref[2 * p]
            sc = sc + jnp.maximum(db[p * QB:(p + 1) * QB], 0.0) * wb_ref[2 * p + 1]
        col = c * KC + lax.broadcasted_iota(I32, (QB, KC), 1)
        key_ref[c] = jnp.where(col <= rowpos, _sortable(sc), INT_MIN)
        return carry

    lax.fori_loop(0, nkc, idx_body, 0)

    def count_ge(cand):
        def body(c, acc):
            return acc + (key_ref[c] >= cand).astype(I32)
        acc = lax.fori_loop(0, nkc, body, jnp.zeros((QB, KC), I32))
        return jnp.sum(acc, axis=1, keepdims=True)

    thr = jnp.maximum(_kth_largest(count_ge, (QB, 1), TOPK), INT_MIN + 1)

    m_ref[...] = jnp.full(m_ref.shape, NEG_BIG, F32)
    l_ref[...] = jnp.zeros(l_ref.shape, F32)
    acc_ref[...] = jnp.zeros(acc_ref.shape, F32)

    def att_body(c, carry):
        off = pl.multiple_of(c * KC, KC)
        sel = key_ref[c] >= thr
        for n in range(N_KV):
            kc = k_ref[pl.ds(off, KC), n * 128:(n + 1) * 128]
            vc = v_ref[pl.ds(off, KC), n * 128:(n + 1) * 128]
            s = lax.dot_general(qs_ref[n], kc, nt, preferred_element_type=F32) * HEAD_DIM ** -0.5
            ps = []
            for g in range(N_HEADS // N_KV):
                h = n * (N_HEADS // N_KV) + g
                sg = jnp.where(sel, s[g * QB:(g + 1) * QB], NEG_BIG)
                m_old = m_ref[h]
                m_new = jnp.maximum(m_old, jnp.max(sg, axis=1, keepdims=True))
                pg = jnp.where(sel, jnp.exp(sg - m_new), 0.0)
                alpha = jnp.exp(m_old - m_new)
                l_ref[h] = alpha * l_ref[h] + jnp.sum(pg, axis=1, keepdims=True)
                acc_ref[h] = alpha * acc_ref[h]
                m_ref[h] = m_new
                ps.append(pg.astype(BF16))
            pv = jnp.dot(jnp.concatenate(ps, axis=0), vc, preferred_element_type=F32)
            for g in range(N_HEADS // N_KV):
                h = n * (N_HEADS // N_KV) + g
                acc_ref[h] += pv[g * QB:(g + 1) * QB]
        return carry

    lax.fori_loop(0, nkc, att_body, 0)

    for h in range(N_HEADS):
        o_ref[:, h * 128:(h + 1) * 128] = (acc_ref[h] / l_ref[h]).astype(o_ref.dtype)


def _dsa_prompt(q_bf, qi_bf, kiwi, k_bf, v_bf):
    nq = TP // QB
    qrow = lambda w: pl.BlockSpec((QB, w), lambda b, i: (b * nq + i, 0))
    seq = lambda w: pl.BlockSpec((TP, w), lambda b, i: (b, 0))
    return pl.pallas_call(
        _dsa_kernel,
        grid=(NB, nq),
        in_specs=[qrow(DSA_Q), qrow(DSA_QI), qrow(128), seq(128), seq(DSA_KV), seq(DSA_KV)],
        out_specs=qrow(DSA_Q),
        out_shape=jax.ShapeDtypeStruct((R, DSA_Q), BF16),
        scratch_shapes=[pltpu.VMEM((2, TP, 128), BF16),
                        pltpu.VMEM((DSA_QI // 128 * QB, 128), BF16),
                        pltpu.VMEM((N_KV, N_HEADS // N_KV * QB, 128), BF16),
                        pltpu.VMEM((N_IDX_HEADS, QB, KC), F32),
                        pltpu.VMEM((NKC, QB, KC), I32),
                        pltpu.VMEM((N_HEADS, QB, 1), F32),
                        pltpu.VMEM((N_HEADS, QB, 1), F32),
                        pltpu.VMEM((N_HEADS, QB, 128), F32)],
        compiler_params=_cparams(("arbitrary", "arbitrary")),
        name="dsa_prompt",
    )(q_bf, qi_bf, kiwi, kiwi, k_bf, v_bf)


PG = 8
NPAGES = PAST // PAGE
NSEL = TOPK + 16


def _sample_prep_kernel(q_ref, k_ref, qi_ref, t_ref, c128_ref, s128_ref, c64_ref, s64_ref,
                        qo_ref, ko_ref, qio_ref, kw_ref, sn_ref):
    c128, s128 = c128_ref[...], s128_ref[...]
    c64, s64 = c64_ref[...], s64_ref[...]
    for h in range(N_HEADS):
        sl = slice(h * 128, (h + 1) * 128)
        x = q_ref[:, sl]
        qo_ref[:, sl] = x * c128 + _rot128(x) * s128
    for h in range(N_KV):
        sl = slice(h * 128, (h + 1) * 128)
        x = k_ref[:, sl]
        ko_ref[:, sl] = x * c128 + _rot128(x) * s128
    t = t_ref[...]
    lane = lax.broadcasted_iota(I32, t.shape, 1)
    kw = jnp.where(lane < D_IDX, t * c64 + _rot64(t) * s64, t)
    kw_ref[...] = kw
    ka = jnp.where(lane < D_IDX, kw, 0.0).astype(BF16).astype(F32)
    kb = pltpu.roll(ka, 64, 1)
    sn = jnp.zeros((NS, 1), F32)
    for p in range(DSA_QI // 128):
        sl = slice(p * 128, (p + 1) * 128)
        x = qi_ref[:, sl]
        r = x * c64 + _rot64(x) * s64
        qio_ref[:, sl] = r
        rb = r.astype(BF16).astype(F32)
        da = jnp.sum(rb * ka, axis=1, keepdims=True)
        db = jnp.sum(rb * kb, axis=1, keepdims=True)
        sn = sn + jnp.maximum(da, 0.0) * (kw[:, D_IDX + 2 * p:D_IDX + 2 * p + 1] * IDX_SCALE)
        sn = sn + jnp.maximum(db, 0.0) * (kw[:, D_IDX + 2 * p + 1:D_IDX + 2 * p + 2] * IDX_SCALE)
    sn_ref[...] = jnp.broadcast_to(sn, (NS, 128))


def _sample_prep(p, tail):
    tabs = [t.reshape(1, 128) for t in _rope_tables(jnp.full((1,), PAST, I32))]
    sb = S0 // NS
    row = lambda w, j: pl.BlockSpec((NS, w), lambda i: (sb, j))
    full = lambda w: pl.BlockSpec((NS, w), lambda i: (0, 0))
    tab = pl.BlockSpec((1, 128), lambda i: (0, 0))
    return pl.pallas_call(
        _sample_prep_kernel,
        grid=(1,),
        in_specs=[row(DSA_Q, 0), row(DSA_KV, 4), row(DSA_QI, 3), row(128, 0), tab, tab, tab, tab],
        out_specs=[full(DSA_Q), full(DSA_KV), full(DSA_QI), full(128), full(128)],
        out_shape=[jax.ShapeDtypeStruct((NS, DSA_Q), F32), jax.ShapeDtypeStruct((NS, DSA_KV), F32),
                   jax.ShapeDtypeStruct((NS, DSA_QI), F32), jax.ShapeDtypeStruct((NS, 128), F32),
                   jax.ShapeDtypeStruct((NS, 128), F32)],
        compiler_params=_cparams(("arbitrary",)),
        name="sample_prep",
    )(p, p, p, tail, *tabs)


def _sample_scores_kernel(pt_ref, qi_ref, w_ref, *refs):
    del pt_ref
    page_refs, out_ref = refs[:PG], refs[PG]
    qm = qi_ref[...].astype(BF16)
    w = w_ref[...] * IDX_SCALE
    for u in range(PG):
        dots = jnp.dot(qm, page_refs[u][...].astype(BF16), preferred_element_type=F32)
        out_ref[u:u + 1, :] = jnp.sum(jnp.maximum(dots, 0.0) * w, axis=0, keepdims=True)


def _sample_scores(qi_r, kiwi_r, cache_idx_k, page_table, layer):
    qi3 = qi_r.reshape(NS, N_IDX_HEADS, D_IDX)
    w3 = kiwi_r[:, D_IDX:D_IDX + N_IDX_HEADS].reshape(NS, N_IDX_HEADS, 1)

    def page_spec(u):
        return pl.BlockSpec((None, None, D_IDX, PAGE), lambda b, g, pt: (layer, pt[b, g * PG + u], 0, 0))

    cache_t = jnp.swapaxes(cache_idx_k, 2, 3)

    grid_spec = pltpu.PrefetchScalarGridSpec(
        num_scalar_prefetch=1,
        grid=(NS, NPAGES // PG),
        in_specs=[pl.BlockSpec((None, N_IDX_HEADS, D_IDX), lambda b, g, pt: (b, 0, 0)),
                  pl.BlockSpec((None, N_IDX_HEADS, 1), lambda b, g, pt: (b, 0, 0))]
                 + [page_spec(u) for u in range(PG)],
        out_specs=pl.BlockSpec((None, PG, PAGE), lambda b, g, pt: (b, g, 0)),
    )
    return pl.pallas_call(
        _sample_scores_kernel,
        grid_spec=grid_spec,
        out_shape=jax.ShapeDtypeStruct((NS, NPAGES, PAGE), F32),
        compiler_params=_cparams(("arbitrary", "arbitrary")),
        name="sample_scores",
    )(page_table, qi3, w3, *([cache_t] * PG))


def _sample_select_kernel(sc_ref, sn_ref, idx_ref, meta_ref, pos_ref):
    keys = _sortable(sc_ref[...])
    knew = _sortable(sn_ref[...])[:, 0:1]

    def count_ge(cand):
        c = jnp.sum((keys >= cand).astype(I32), axis=1, keepdims=True)
        return jnp.sum(c, axis=0, keepdims=True) + (knew >= cand).astype(I32)

    thr = _kth_largest(count_ge, (1, 1), TOPK)
    sel = keys >= thr
    sel_new = (knew >= thr).astype(I32)
    r = lax.broadcasted_iota(I32, (PAGE, PAGE), 0)
    c = lax.broadcasted_iota(I32, (PAGE, PAGE), 1)
    upper = (r <= c).astype(F32).astype(BF16)
    lower = (c < r).astype(F32).astype(BF16)
    cum = jnp.dot(sel.astype(F32).astype(BF16), upper, preferred_element_type=F32)
    off = jnp.dot(lower, cum.astype(BF16), preferred_element_type=F32)[:, PAGE - 1:PAGE]
    pos_ref[...] = jnp.where(sel, off + cum - 1.0, -1.0)
    nsel = jnp.sum(cum[:, PAGE - 1:PAGE], axis=0, keepdims=True)

    jio = lax.broadcasted_iota(I32, (TOPK, PAGE), 0).astype(F32)
    lane = lax.broadcasted_iota(I32, (TOPK, PAGE), 1)

    def body(p, acc):
        return acc + jnp.where(jio == pos_ref[pl.ds(p, 1), :], p * PAGE + lane, 0)

    acc = lax.fori_loop(0, NPAGES, body, jnp.zeros((TOPK, PAGE), I32))
    idx_ref[...] = jnp.sum(acc, axis=1, keepdims=True)
    ml = lax.broadcasted_iota(I32, (1, 128), 1)
    nsel_i = jnp.minimum(nsel, float(TOPK)).astype(I32)
    meta_ref[...] = jnp.where(ml == 0, nsel_i, jnp.where(ml == 1, sel_new, 0))


def _sample_select(scores, snew):
    return pl.pallas_call(
        _sample_select_kernel,
        grid=(NS,),
        in_specs=[pl.BlockSpec((None, NPAGES, PAGE), lambda b: (b, 0, 0)),
                  pl.BlockSpec((None, 1, 128), lambda b: (b, 0, 0))],
        out_specs=[pl.BlockSpec((None, TOPK, 1), lambda b: (b, 0, 0)),
                   pl.BlockSpec((None, 1, 128), lambda b: (b, 0, 0))],
        out_shape=[jax.ShapeDtypeStruct((NS, TOPK, 1), I32), jax.ShapeDtypeStruct((NS, 1, 128), I32)],
        scratch_shapes=[pltpu.VMEM((NPAGES, PAGE), F32)],
        compiler_params=_cparams(("arbitrary",)),
        name="sample_select",
    )(scores, snew.reshape(NS, 1, 128))


def _sample_attend_kernel(idx_ref, meta_ref, pt_ref, q_ref, kn_ref, vn_ref, ck_ref, cv_ref, o_ref,
                          kbuf, vbuf, sem, *, layer):
    b = pl.program_id(0)

    def row_copies(j, page, slot):
        return (pltpu.make_async_copy(ck_ref.at[layer, page, slot], kbuf.at[j], sem.at[0]),
                pltpu.make_async_copy(cv_ref.at[layer, page, slot], vbuf.at[j], sem.at[1]))

    def issue(j, carry):
        ix = idx_ref[b * TOPK + j]
        page = pt_ref[b * NPAGES + lax.shift_right_logical(ix, 7)]
        for cp in row_copies(j, page, ix & (PAGE - 1)):
            cp.start()
        return carry

    lax.fori_loop(0, TOPK, issue, 0)
    pad = jnp.zeros((NSEL - TOPK - 1, N_KV, HEAD_DIM), F32)
    k_new, v_new = kn_ref[pl.ds(b, 1), :], vn_ref[pl.ds(b, 1), :]
    for n in range(N_KV):
        kbuf[TOPK:TOPK + 1, n, :] = k_new[:, n * 128:(n + 1) * 128]
        vbuf[TOPK:TOPK + 1, n, :] = v_new[:, n * 128:(n + 1) * 128]
    kbuf[TOPK + 1:] = pad
    vbuf[TOPK + 1:] = pad

    def drain(j, carry):
        for cp in row_copies(j, 0, 0):
            cp.wait()
        return carry

    lax.fori_loop(0, TOPK, drain, 0)

    jio = lax.broadcasted_iota(I32, (1, NSEL), 1)
    valid = (jio < meta_ref[2 * b]) | ((jio == TOPK) & (meta_ref[2 * b + 1] > 0))
    nt = (((1,), (1,)), ((), ()))
    hg = N_HEADS // N_KV
    for n in range(N_KV):
        qn = q_ref[n * hg:(n + 1) * hg, :].astype(BF16)
        kb = kbuf[:, n, :].astype(BF16)
        vb = vbuf[:, n, :].astype(BF16)
        s = lax.dot_general(qn, kb, nt, preferred_element_type=F32) * HEAD_DIM ** -0.5
        s = jnp.where(valid, s, NEG_BIG)
        e = jnp.where(valid, jnp.exp(s - jnp.max(s, axis=1, keepdims=True)), 0.0)
        pn = e / jnp.sum(e, axis=1, keepdims=True)
        o_ref[n * hg:(n + 1) * hg, :] = jnp.dot(pn.astype(BF16), vb, preferred_element_type=F32)


def _sample_attend(idx, meta, page_table, q_r, k_r, p, cache_k, cache_v, layer):
    grid_spec = pltpu.PrefetchScalarGridSpec(
        num_scalar_prefetch=3,
        grid=(NS,),
        in_specs=[pl.BlockSpec((None, N_HEADS, HEAD_DIM), lambda b, *_: (b, 0, 0)),
                  pl.BlockSpec((NS, DSA_KV), lambda b, *_: (0, 0)),
                  pl.BlockSpec((NS, DSA_KV), lambda b, *_: (S0 // NS, 5)),
                  pl.BlockSpec(memory_space=pl.ANY),
                  pl.BlockSpec(memory_space=pl.ANY)],
        out_specs=pl.BlockSpec((None, N_HEADS, HEAD_DIM), lambda b, *_: (b, 0, 0)),
        scratch_shapes=[pltpu.VMEM((NSEL, N_KV, HEAD_DIM), F32), pltpu.VMEM((NSEL, N_KV, HEAD_DIM), F32),
                        pltpu.SemaphoreType.DMA((2,))],
    )
    return pl.pallas_call(
        functools.partial(_sample_attend_kernel, layer=layer),
        grid_spec=grid_spec,
        out_shape=jax.ShapeDtypeStruct((NS, N_HEADS, HEAD_DIM), F32),
        compiler_params=_cparams(("arbitrary",)),
        name="sample_attend",
    )(idx.reshape(NS * TOPK), meta[:, 0, :2].reshape(NS * 2), page_table.reshape(NS * NPAGES),
      q_r.reshape(NS, N_HEADS, HEAD_DIM), k_r, p, cache_k, cache_v)


def _mixer_b(x, cache_k, cache_v, cache_idx_k, page_table, w_in, layer):
    wt = jnp.swapaxes(w_in, 1, 2)
    p = _project_t(x, wt, layer, 0, DSA_MAIN, 1024)
    tail = _project_t(x, wt, layer, DSA_MAIN, 128, 128)
    q_bf, k_f, k_bf, v_bf, qi_bf, kiwi = _rope_prompt(p, tail)
    o = _dsa_prompt(q_bf, qi_bf, kiwi, k_bf, v_bf)
    q_r, k_r, qi_r, kiwi_r, snew = _sample_prep(p, tail)
    scores = _sample_scores(qi_r, kiwi_r, cache_idx_k, page_table, layer)
    idx, meta = _sample_select(scores, snew)
    o_s = _sample_attend(idx, meta, page_table, q_r, k_r, p, cache_k, cache_v, layer)
    o = lax.dynamic_update_slice(o, o_s.reshape(NS, DSA_Q).astype(BF16), (S0, 0))
    p3 = p.reshape(NB, TP, DSA_MAIN)
    outs = dict(
        k_p=k_f.reshape(NB, TP, N_KV, HEAD_DIM)[:, :T_REAL],
        v_p=p3[:, :T_REAL, DSA_Q + DSA_KV:DSA_Q + 2 * DSA_KV].reshape(NB, T_REAL, N_KV, HEAD_DIM),
        ki_p=kiwi.reshape(NB, TP, 128)[:, :T_REAL, :D_IDX],
        k_s=k_r.reshape(NS, 1, N_KV, HEAD_DIM),
        v_s=p[S0:S0 + NS, DSA_Q + DSA_KV:DSA_Q + 2 * DSA_KV].reshape(NS, 1, N_KV, HEAD_DIM),
        ki_s=kiwi_r[:, :D_IDX].reshape(NS, 1, D_IDX))
    return o, outs


DN_MAIN = DN_CONV_DIM + DN_VW
NEG_INF = float("-inf")


def _silu(x):
    return x * jax.nn.sigmoid(x)


def _softplus(x):
    return jnp.maximum(x, 0.0) + jnp.log1p(jnp.exp(-jnp.abs(x)))


def _conv_c_kernel(x_ref, w_ref, o_ref):
    c = pl.program_id(1)
    x = x_ref[...]
    row = lax.broadcasted_iota(I32, x.shape, 0)
    w = w_ref[...]
    y = w[3:4] * x
    for j in range(1, 4):
        y = y + w[3 - j:4 - j] * jnp.where(row >= j, pltpu.roll(x, j, 0), 0.0)
    s = _silu(y)
    nrm = s * lax.rsqrt(jnp.sum(jnp.square(s), -1, keepdims=True) + RMS_EPS)
    is_q = c < DN_QK_HEADS
    is_qk = c < 2 * DN_QK_HEADS
    o_ref[...] = jnp.where(is_qk, nrm * jnp.where(is_q, DN_DK ** -0.5, 1.0), s)


def _conv_c_prompt(p, conv_w, layer):
    p3 = p.reshape(NB, TP, DN_MAIN)
    return pl.pallas_call(
        _conv_c_kernel,
        grid=(NB, DN_CONV_DIM // 128),
        in_specs=[pl.BlockSpec((None, TP, 128), lambda b, c: (b, 0, c)),
                  pl.BlockSpec((None, 4, 128), lambda b, c: (layer, 0, c))],
        out_specs=pl.BlockSpec((None, TP, 128), lambda b, c: (b, 0, c)),
        out_shape=jax.ShapeDtypeStruct((NB, TP, DN_CONV_DIM), F32),
        compiler_params=_cparams(("arbitrary", "arbitrary")),
        name="conv_c_prompt",
    )(p3, conv_w)


def _gate_lanes(a_log, dt_bias, layer):
    pad = lambda v: jnp.pad(v[layer], (DN_V_HEADS, 128 - 2 * DN_V_HEADS)).reshape(1, 128)
    return pad(a_log), pad(dt_bias)


def _gates_kernel(t_ref, al_ref, dt_ref, o_ref):
    i = pl.program_id(0)
    t = t_ref[...]
    row = lax.broadcasted_iota(I32, t.shape, 0)
    lane = lax.broadcasted_iota(I32, t.shape, 1)
    real = (i % (TP // TR)) * TR + row < T_REAL
    beta = jnp.where(real, jax.nn.sigmoid(t), 0.0)
    g = jnp.where(real, -jnp.exp(al_ref[...]) * _softplus(t + dt_ref[...]), 0.0)
    pos = row & (CH - 1)
    sh = 1
    while sh < CH:
        g = g + jnp.where(pos >= sh, pltpu.roll(g, sh, 0), 0.0)
        sh *= 2
    o_ref[...] = jnp.where(lane < DN_V_HEADS, beta, g)


def _gates(tail, a_log, dt_bias, layer):
    al, dt = _gate_lanes(a_log, dt_bias, layer)
    vec = pl.BlockSpec((1, 128), lambda i: (0, 0))
    return pl.pallas_call(
        _gates_kernel,
        grid=(R // TR,),
        in_specs=[pl.BlockSpec((TR, 128), lambda i: (i, 0)), vec, vec],
        out_specs=pl.BlockSpec((TR, 128), lambda i: (i, 0)),
        out_shape=jax.ShapeDtypeStruct((R, 128), F32),
        compiler_params=_cparams(("arbitrary",)),
        name="dn_gates",
    )(tail, al, dt)


def _delta_kernel(q_ref, k_ref, v_ref, z_ref, g_ref, gr_ref, nw_ref, o_ref, s_ref):
    c = pl.program_id(1)

    @pl.when(c == 0)
    def _():
        s_ref[...] = jnp.zeros(s_ref.shape, F32)

    rows = HG * CH
    nt = (((1,), (1,)), ((), ()))
    tn = (((0,), (0,)), ((), ()))
    gates = g_ref[...]
    nw = nw_ref[...]
    ri = lax.broadcasted_iota(I32, (rows, rows), 0)
    ci = lax.broadcasted_iota(I32, (rows, rows), 1)
    same = (ri // CH) == (ci // CH)
    causal = same & (ci <= ri)
    strict = same & (ci < ri)
    rb = lax.broadcasted_iota(I32, (rows, HG * DN_DV), 0) // CH
    cb = lax.broadcasted_iota(I32, (rows, HG * DN_DV), 1) // DN_DV
    diag = rb == cb

    for g in range(NG):
        heads = [HG * g + hh for hh in range(HG)]
        stack = lambda ref, idx: jnp.concatenate([ref[:, j * 128:(j + 1) * 128] for j in idx], axis=0)
        kst = stack(k_ref, [h // 2 for h in heads])
        qst = stack(q_ref, [h // 2 for h in heads])
        vst = stack(v_ref, heads)
        beta = jnp.concatenate([gates[:, h:h + 1] for h in heads], axis=0)
        gc = jnp.concatenate([gates[:, DN_V_HEADS + h:DN_V_HEADS + h + 1] for h in heads], axis=0)
        gl = jnp.concatenate([jnp.broadcast_to(gates[CH - 1:CH, DN_V_HEADS + h:DN_V_HEADS + h + 1], (CH, 1))
                              for h in heads], axis=0)
        gcrow = gr_ref[g:g + 1, :]
        glrow = jnp.concatenate([jnp.broadcast_to(gcrow[:, hh * CH + CH - 1:hh * CH + CH], (1, DN_DV))
                                 for hh in range(HG)], axis=1)

        kb = kst.astype(BF16)
        kk = lax.dot_general(kb, kb, nt, preferred_element_type=F32)
        qk = lax.dot_general(qst.astype(BF16), kb, nt, preferred_element_type=F32)
        decay = jnp.where(causal, jnp.exp(jnp.where(causal, gc - gcrow, 0.0)), 0.0)
        a = jnp.where(strict, beta * decay * kk, 0.0)
        pmat = (qk * decay).astype(BF16)
        eg = jnp.exp(gc)
        x = jnp.concatenate([beta * vst, (beta * eg) * kst], axis=1)
        pw = a.astype(BF16)
        x = x - jnp.dot(pw, x.astype(BF16), preferred_element_type=F32)
        for _ in range(5):
            p2 = jnp.dot(pw, pw, preferred_element_type=F32)
            pw = p2.astype(BF16)
            x = x + jnp.dot(pw, x.astype(BF16), preferred_element_type=F32)
        u, w = x[:, :DN_DV], x[:, DN_DV:]

        s = s_ref[g]
        sb = s.astype(BF16)
        ws = jnp.dot(w.astype(BF16), sb, preferred_element_type=F32)
        u4 = jnp.concatenate([u] * HG, axis=1)
        delta = jnp.where(diag, u4 - ws, 0.0).astype(BF16)
        qs = jnp.dot((qst * eg).astype(BF16), sb, preferred_element_type=F32)
        obd = jnp.where(diag, qs, 0.0) + jnp.dot(pmat, delta, preferred_element_type=F32)
        ost = obd[:, :DN_DV]
        for hh in range(1, HG):
            ost = ost + obd[:, hh * DN_DV:(hh + 1) * DN_DV]
        kd = (kst * jnp.exp(gl - gc)).astype(BF16)
        s_ref[g] = jnp.exp(glrow) * s + lax.dot_general(kd, delta, tn, preferred_element_type=F32)

        for hh, h in enumerate(heads):
            oh = ost[hh * CH:(hh + 1) * CH]
            oh = oh * lax.rsqrt(jnp.mean(jnp.square(oh), -1, keepdims=True) + RMS_EPS) * nw
            o_ref[:, h * 128:(h + 1) * 128] = (oh * _silu(z_ref[:, h * 128:(h + 1) * 128])).astype(o_ref.dtype)


def _delta_prompt(qkvc, p, gates, norm_w, layer):
    p3 = p.reshape(NB, TP, DN_MAIN)
    g3 = gates.reshape(NB, TP, 128)
    gcrow = g3[:, :, DN_V_HEADS:2 * DN_V_HEADS].reshape(NB, NCH, CH, DN_V_HEADS)
    gcrow = jnp.swapaxes(gcrow, 2, 3).reshape(NB, NCH, NG, HG * CH)
    blk = lambda w, j: pl.BlockSpec((None, CH, w), lambda b, c: (b, c, j))
    o, s = pl.pallas_call(
        _delta_kernel,
        grid=(NB, NCH),
        in_specs=[blk(DN_QK, 0), blk(DN_QK, 1), blk(DN_VW, 1), blk(DN_VW, 2), blk(128, 0),
                  pl.BlockSpec((None, None, NG, HG * CH), lambda b, c: (b, c, 0, 0)),
                  pl.BlockSpec((None, 1, DN_DV), lambda b, c: (layer, 0, 0))],
        out_specs=[blk(DN_VW, 0),
                   pl.BlockSpec((None, NG, DN_DK, HG * DN_DV), lambda b, c: (b, 0, 0, 0))],
        out_shape=[jax.ShapeDtypeStruct((NB, TP, DN_VW), BF16),
                   jax.ShapeDtypeStruct((NB, NG, DN_DK, HG * DN_DV), F32)],
        compiler_params=_cparams(("arbitrary", "arbitrary")),
        name="delta_prompt",
    )(qkvc, qkvc, qkvc, p3, g3, gcrow, norm_w.reshape(-1, 1, DN_DV))
    s = s.reshape(NB, NG, DN_DK, HG, DN_DV)
    s = jnp.transpose(s, (0, 1, 3, 2, 4)).reshape(NB, DN_V_HEADS, DN_DK, DN_DV)
    return o.reshape(R, DN_VW), s


def _delta_sample_kernel(xq_ref, xk_ref, xv_ref, z_ref, sq_ref, sk_ref, sv_ref, wq_ref, wk_ref, wv_ref,
                         t_ref, al_ref, dt_ref, nw_ref, s_ref, o_ref, so_ref):
    hv = pl.program_id(0)

    def conv(st_ref, x_ref, w_ref):
        w = w_ref[...]
        y = w[0:1] * st_ref[0] + w[1:2] * st_ref[1] + w[2:3] * st_ref[2] + w[3:4] * x_ref[...]
        return _silu(y)

    q = conv(sq_ref, xq_ref, wq_ref)
    k = conv(sk_ref, xk_ref, wk_ref)
    v = conv(sv_ref, xv_ref, wv_ref)
    q = q * lax.rsqrt(jnp.sum(jnp.square(q), -1, keepdims=True) + RMS_EPS) * DN_DK ** -0.5
    k = k * lax.rsqrt(jnp.sum(jnp.square(k), -1, keepdims=True) + RMS_EPS)
    t = t_ref[...]
    lane = lax.broadcasted_iota(I32, t.shape, 1)
    beta_all = jax.nn.sigmoid(t)
    g_all = -jnp.exp(al_ref[...]) * _softplus(t + dt_ref[...])
    beta = jnp.sum(jnp.where(lane == hv, beta_all, 0.0), axis=1, keepdims=True)
    decay = jnp.exp(jnp.sum(jnp.where(lane == DN_V_HEADS + hv, g_all, 0.0), axis=1, keepdims=True))
    eye = lax.broadcasted_iota(I32, (DN_DK, DN_DK), 0) == lax.broadcasted_iota(I32, (DN_DK, DN_DK), 1)
    to_col = lambda r: jnp.sum(jnp.where(eye, r, 0.0), axis=1, keepdims=True)
    nw = nw_ref[...]
    z = z_ref[...]
    for b in range(NS):
        s = s_ref[b] * decay[b:b + 1]
        kcol = to_col(k[b:b + 1])
        v_old = jnp.sum(s * kcol, axis=0, keepdims=True)
        s = s + kcol * (beta[b:b + 1] * (v[b:b + 1] - v_old))
        so_ref[b] = s
        o = jnp.sum(s * to_col(q[b:b + 1]), axis=0, keepdims=True)
        o = o * lax.rsqrt(jnp.mean(jnp.square(o), -1, keepdims=True) + RMS_EPS) * nw
        o_ref[b:b + 1, :] = o * _silu(z[b:b + 1])


def _delta_sample(p, tail, conv_state, state, conv_w, a_log, dt_bias, norm_w, layer):
    st = jnp.swapaxes(conv_state[layer], 0, 1)
    al, dt = _gate_lanes(a_log, dt_bias, layer)
    sb = S0 // NS
    nq, nqk = DN_QK_HEADS, 2 * DN_QK_HEADS
    xrow = lambda f: pl.BlockSpec((NS, 128), lambda h: (sb, f(h)))
    srow = lambda f: pl.BlockSpec((3, NS, 128), lambda h: (0, 0, f(h)))
    wrow = lambda f: pl.BlockSpec((None, 4, 128), lambda h: (layer, 0, f(h)))
    fq, fk, fv = (lambda h: h // 2), (lambda h: nq + h // 2), (lambda h: nqk + h)
    vec = pl.BlockSpec((1, 128), lambda h: (0, 0))
    sspec = pl.BlockSpec((None, NS, None, DN_DK, DN_DV), lambda h: (layer, 0, h, 0, 0))
    return pl.pallas_call(
        _delta_sample_kernel,
        grid=(DN_V_HEADS,),
        in_specs=[xrow(fq), xrow(fk), xrow(fv), xrow(lambda h: nqk + DN_V_HEADS + h),
                  srow(fq), srow(fk), srow(fv), wrow(fq), wrow(fk), wrow(fv),
                  pl.BlockSpec((NS, 128), lambda h: (sb, 0)), vec, vec,
                  pl.BlockSpec((None, 1, DN_DV), lambda h: (layer, 0, 0)), sspec],
        out_specs=[pl.BlockSpec((NS, 128), lambda h: (0, h)),
                   pl.BlockSpec((NS, None, DN_DK, DN_DV), lambda h: (0, h, 0, 0))],
        out_shape=[jax.ShapeDtypeStruct((NS, DN_VW), F32),
                   jax.ShapeDtypeStruct((NS, DN_V_HEADS, DN_DK, DN_DV), F32)],
        compiler_params=_cparams(("arbitrary",)),
        name="delta_sample",
    )(p, p, p, p, st, st, st, conv_w, conv_w, conv_w, tail, al, dt, norm_w.reshape(-1, 1, DN_DV), state)


def _mixer_c(x, conv_state, state, w_in, conv_w, a_log, dt_bias, norm_w, layer):
    wt = jnp.swapaxes(w_in, 1, 2)
    p = _project_t(x, wt, layer, 0, DN_MAIN, 1024)
    tail = _project_t(x, wt, layer, DN_MAIN, 128, 128)
    qkvc = _conv_c_prompt(p, conv_w, layer)
    gates = _gates(tail, a_log, dt_bias, layer)
    o, s_p = _delta_prompt(qkvc, p, gates, norm_w, layer)
    o_s, s_s = _delta_sample(p, tail, conv_state, state, conv_w, a_log, dt_bias, norm_w, layer)
    o = lax.dynamic_update_slice(o, o_s.astype(BF16), (S0, 0))
    p3 = p.reshape(NB, TP, DN_MAIN)
    outs = dict(
        conv_p=p3[:, T_REAL - 3:T_REAL, :DN_CONV_DIM],
        conv_s=jnp.concatenate([conv_state[layer][:, 1:], p[S0:S0 + NS, None, :DN_CONV_DIM]], axis=1),
        s_p=s_p, s_s=s_s)
    return o, outs


def _build_x(x_prompt, x_sample, meta_tokens):
    meta = jnp.broadcast_to(meta_tokens[None], (NB, N_META, D))
    xp = jnp.concatenate([meta, x_prompt, jnp.zeros((NB, TP - T_REAL, D), F32)], axis=1)
    x = xp.reshape(R, D)
    return lax.dynamic_update_slice(x, x_sample.reshape(NS, D), (S0, 0))


def kernel(x_prompt, x_sample, state_conv_a, cache_k, cache_v, cache_idx_k, state_conv_c, state_delta, page_table, meta_tokens, w_in_a, conv_w_a, w_out_a, w_in_b, w_out_b, w_in_c, conv_w_c, a_log_c, dt_bias_c, norm_w_c, w_out_c, ln1_g, ln1_b, w1, w2, ln2_g, ln2_b):
    x = _build_x(x_prompt, x_sample, meta_tokens)
    conv_a_p, conv_a_s = [], []
    b_outs, c_outs = [], []
    for i in range(DEPTH):
        j, kind = i // 3, i % 3
        if kind == 0:
            y, st_p, st_s = _mixer_a(x, state_conv_a[j], w_in_a, conv_w_a, j)
            conv_a_p.append(st_p)
            conv_a_s.append(st_s)
            w_out = w_out_a
        elif kind == 1:
            y, outs = _mixer_b(x, cache_k, cache_v, cache_idx_k, page_table, w_in_b, j)
            b_outs.append(outs)
            w_out = w_out_b
        else:
            y, outs = _mixer_c(x, state_conv_c, state_delta, w_in_c, conv_w_c, a_log_c, dt_bias_c,
                               norm_w_c, j)
            c_outs.append(outs)
            w_out = w_out_c
        x = _out_ln(y, w_out, j, x, ln1_g, ln1_b, i)
        x = _mlp_ln(x, w1, w2, ln2_g, ln2_b, i)
    x3 = x.reshape(NB, TP, D)
    stack = lambda outs, name: jnp.stack([o[name] for o in outs])
    return (x3[:, N_META:T_REAL], x[S0:S0 + NS].reshape(NS, 1, D),
            jnp.stack(conv_a_p), jnp.stack(conv_a_s),
            stack(b_outs, "k_p"), stack(b_outs, "v_p"), stack(b_outs, "ki_p"),
            stack(b_outs, "k_s"), stack(b_outs, "v_s"), stack(b_outs, "ki_s"),
            stack(c_outs, "conv_p"), stack(c_outs, "conv_s"),
            stack(c_outs, "s_p"), stack(c_outs, "s_s"))
```
